```python
import jax, jax.numpy as jnp
from jax import lax
import numpy as np

D_MODEL = 1024
BATCH = 8
SEQ = 2048
DEPTH = 2

GRID_W = 64
N_MIXERS = 2
HEAD_DIM = 64
A_HEADS = D_MODEL // HEAD_DIM
A_GROUPS = ((128, 1), (512, 4), (2048, 16))
B_HEADS = D_MODEL // HEAD_DIM
B_KV_HEADS = B_HEADS // 4
Q_BLOCK = 128
ROPE_THETA = 10000.0
N_EXPERTS = 32
TOP_K = 4
D_FF = D_MODEL
SWIGLU_LIMIT = 7.0
SWIGLU_ALPHA = 1.702
NORM_EPS = 1e-6

kernel_name = "hybrid_dilated_axial_gqa_moe_encoder"


def rms_norm(x, g):
    xf = x.astype(jnp.float32)
    y = xf * lax.rsqrt(jnp.mean(xf * xf, axis=-1, keepdims=True) + NORM_EPS)
    return (y * g.astype(jnp.float32)).astype(x.dtype)


def rope_tables(pos, dim):
    inv = 1.0 / (ROPE_THETA ** (jnp.arange(0, dim, 2, dtype=jnp.float32) / dim))
    ang = pos.astype(jnp.float32)[:, None] * inv[None, :]
    return jnp.cos(ang), jnp.sin(ang)


def apply_rope(x, cos, sin):
    half = x.shape[-1] // 2
    xf = x.astype(jnp.float32)
    x1, x2 = xf[..., :half], xf[..., half:]
    c, s = cos[:, None, :], sin[:, None, :]
    return jnp.concatenate([x1 * c - x2 * s, x2 * c + x1 * s], axis=-1).astype(x.dtype)


def to_strided(t, d):
    b, s = t.shape[:2]
    t = jnp.swapaxes(t.reshape(b, s // d, d, *t.shape[2:]), 1, 2)
    return t.reshape(b * d, s // d, *t.shape[3:])


def from_strided(t, b, d):
    n, l = t.shape[:2]
    t = jnp.swapaxes(t.reshape(b, d, l, *t.shape[2:]), 1, 2)
    return t.reshape(b, l * d, *t.shape[3:])


def banded_attention(q, k, v, half):
    n, l, h, dh = q.shape
    blk = half
    nb = -(-l // blk)
    lp = nb * blk
    pad = lp - l
    qp = jnp.pad(q, ((0, 0), (0, pad), (0, 0), (0, 0))).reshape(n, nb, blk, h, dh)
    kp = jnp.pad(k, ((0, 0), (blk, pad + blk), (0, 0), (0, 0))).reshape(n, nb + 2, blk, h, dh)
    vp = jnp.pad(v, ((0, 0), (blk, pad + blk), (0, 0), (0, 0))).reshape(n, nb + 2, blk, h, dh)
    kb = jnp.concatenate([kp[:, :-2], kp[:, 1:-1], kp[:, 2:]], axis=2)
    vb = jnp.concatenate([vp[:, :-2], vp[:, 1:-1], vp[:, 2:]], axis=2)
    qpos = jnp.arange(nb)[:, None] * blk + jnp.arange(blk)[None, :]
    kpos = jnp.arange(nb)[:, None] * blk - blk + jnp.arange(3 * blk)[None, :]
    diff = kpos[:, None, :] - qpos[:, :, None]
    valid = (jnp.abs(diff) <= half) & (kpos[:, None, :] >= 0) & (kpos[:, None, :] < l)
    s = jnp.einsum('nbqhd,nbkhd->nbhqk', qp.astype(jnp.float32), kb.astype(jnp.float32)) * (dh ** -0.5)
    s = jnp.where(valid[None, :, None], s, -jnp.inf)
    lse = jax.nn.logsumexp(s, axis=-1, keepdims=True)
    p = jnp.exp(s - lse)
    o = jnp.einsum('nbhqk,nbkhd->nbqhd', p, vb.astype(jnp.float32)).reshape(n, lp, h, dh)[:, :l]
    lse = jnp.transpose(lse[..., 0], (0, 1, 3, 2)).reshape(n, lp, h)[:, :l]
    return o, lse


def dilated_attention(h, w_in, w_out):
    b, s, _ = h.shape
    proj = (h @ w_in).reshape(b, s, len(A_GROUPS), 3, A_HEADS, HEAD_DIM)
    cos, sin = rope_tables(jnp.arange(s), HEAD_DIM)
    outs, lses = [], []
    for g, (window, dil) in enumerate(A_GROUPS):
        q = apply_rope(proj[:, :, g, 0], cos, sin)
        k = apply_rope(proj[:, :, g, 1], cos, sin)
        v = proj[:, :, g, 2]
        o, lse = banded_attention(to_strided(q, dil), to_strided(k, dil), to_strided(v, dil),
                                  window // (2 * dil))
        outs.append(from_strided(o, b, dil))
        lses.append(from_strided(lse, b, dil))
    wts = jax.nn.softmax(jnp.stack(lses, 0), axis=0)
    o = jnp.sum(jnp.stack(outs, 0) * wts[..., None], axis=0)
    return o.astype(h.dtype).reshape(b, s, A_HEADS * HEAD_DIM) @ w_out


def axial_gqa_attention(h, w_in, g_q, g_k, w_out):
    b, s, _ = h.shape
    rows = s // GRID_W
    row_pos = jnp.repeat(jnp.arange(rows), GRID_W)
    col_pos = jnp.tile(jnp.arange(GRID_W), rows)
    qd, kd = B_HEADS * HEAD_DIM, B_KV_HEADS * HEAD_DIM
    proj = h @ w_in
    q = rms_norm(proj[..., :qd].reshape(b, s, B_HEADS, HEAD_DIM), g_q)
    k = rms_norm(proj[..., qd:qd + kd].reshape(b, s, B_KV_HEADS, HEAD_DIM), g_k)
    v = proj[..., qd + kd:].reshape(b, s, B_KV_HEADS, HEAD_DIM)
    half = HEAD_DIM // 2
    cr, sr = rope_tables(row_pos, half)
    cc, sc = rope_tables(col_pos, half)
    q = jnp.concatenate([apply_rope(q[..., :half], cr, sr), apply_rope(q[..., half:], cc, sc)], axis=-1)
    k = jnp.concatenate([apply_rope(k[..., :half], cr, sr), apply_rope(k[..., half:], cc, sc)], axis=-1)
    grp = B_HEADS // B_KV_HEADS
    nq = s // Q_BLOCK
    qb = q.reshape(b, nq, Q_BLOCK, B_KV_HEADS, grp, HEAD_DIM).transpose(1, 0, 2, 3, 4, 5)
    kf = k.astype(jnp.float32)
    vf = v.astype(jnp.float32)
    scale = HEAD_DIM ** -0.5

    def block(qi):
        sc_ = jnp.einsum('bqkgd,bskd->bkgqs', qi.astype(jnp.float32), kf) * scale
        p = jax.nn.softmax(sc_, axis=-1)
        return jnp.einsum('bkgqs,bskd->bqkgd', p, vf).astype(h.dtype)

    o = lax.map(block, qb).transpose(1, 0, 2, 3, 4, 5).reshape(b, s, qd)
    return o @ w_out


def moe_ffn(h, w_router, b_router, w_up, b_up, w_down, b_down):
    b, s, d = h.shape
    t = h.reshape(b * s, d)
    logits = (t @ w_router + b_router).astype(jnp.float32)
    top_v, top_i = lax.top_k(logits, TOP_K)
    gates = jax.nn.softmax(top_v, axis=-1)
    combine = jnp.sum(jax.nn.one_hot(top_i, N_EXPERTS, dtype=jnp.float32) * gates[..., None], axis=1)
    combine = combine.astype(t.dtype)
    y = jnp.zeros_like(t)
    for e in range(N_EXPERTS):
        hu = t @ w_up[e] + b_up[e]
        x_glu = jnp.minimum(hu[..., :D_FF], SWIGLU_LIMIT)
        x_lin = jnp.clip(hu[..., D_FF:], -SWIGLU_LIMIT, SWIGLU_LIMIT)
        act = x_glu * jax.nn.sigmoid(SWIGLU_ALPHA * x_glu) * (x_lin + 1.0)
        y = y + combine[:, e:e + 1] * (act @ w_down[e] + b_down[e])
    return y.reshape(b, s, d)


def modulation(c, w, bias):
    m = jax.nn.silu(c) @ w + bias
    shift, scale, gate = jnp.split(m, 3, axis=-1)
    return shift[:, None, :], scale[:, None, :], gate[:, None, :]


def setup_inputs(seed: int = 0) -> dict:
    key = jax.random.key(seed)
    ks = jax.random.split(key, 24)
    d = D_MODEL
    n_a = len(range(0, DEPTH, N_MIXERS))
    n_b = len(range(1, DEPTH, N_MIXERS))
    a_cols = len(A_GROUPS) * 3 * A_HEADS * HEAD_DIM
    b_cols = (B_HEADS + 2 * B_KV_HEADS) * HEAD_DIM

    def nrm(k, shape, std):
        return jax.random.normal(k, shape, jnp.float32) * std

    return {
        "x": nrm(ks[0], (BATCH, SEQ, d), 1.0),
        "c": nrm(ks[1], (BATCH, d), 1.0),
        "w_mod": nrm(ks[2], (DEPTH, 2, d, 3 * d), 0.5 * d ** -0.5),
        "b_mod": nrm(ks[3], (DEPTH, 2, 3 * d), 0.02),
        "g_pre_mix": 1.0 + nrm(ks[4], (DEPTH, d), 0.05),
        "g_post_mix": 1.0 + nrm(ks[5], (DEPTH, d), 0.05),
        "g_pre_ffn": 1.0 + nrm(ks[6], (DEPTH, d), 0.05),
        "g_post_ffn": 1.0 + nrm(ks[7], (DEPTH, d), 0.05),
        "w_in_a": nrm(ks[8], (n_a, d, a_cols), d ** -0.5),
        "w_out_a": nrm(ks[9], (n_a, A_HEADS * HEAD_DIM, d), (A_HEADS * HEAD_DIM) ** -0.5),
        "w_in_b": nrm(ks[10], (n_b, d, b_cols), d ** -0.5),
        "g_q_b": 1.0 + nrm(ks[11], (n_b, HEAD_DIM), 0.05),
        "g_k_b": 1.0 + nrm(ks[12], (n_b, HEAD_DIM), 0.05),
        "w_out_b": nrm(ks[13], (n_b, B_HEADS * HEAD_DIM, d), (B_HEADS * HEAD_DIM) ** -0.5),
        "w_router": nrm(ks[14], (DEPTH, d, N_EXPERTS), d ** -0.5),
        "b_router": nrm(ks[15], (DEPTH, N_EXPERTS), 0.01),
        "w_up": nrm(ks[16], (DEPTH, N_EXPERTS, d, 2 * D_FF), d ** -0.5),
        "b_up": nrm(ks[17], (DEPTH, N_EXPERTS, 2 * D_FF), 0.02),
        "w_down": nrm(ks[18], (DEPTH, N_EXPERTS, D_FF, d), D_FF ** -0.5),
        "b_down": nrm(ks[19], (DEPTH, N_EXPERTS, d), 0.02),
    }


def reference(x, c, w_mod, b_mod, g_pre_mix, g_post_mix, g_pre_ffn, g_post_ffn,
              w_in_a, w_out_a, w_in_b, g_q_b, g_k_b, w_out_b,
              w_router, b_router, w_up, b_up, w_down, b_down):
    for i in range(DEPTH):
        j = i // N_MIXERS
        shift, scale, gate = modulation(c, w_mod[i, 0], b_mod[i, 0])
        h = rms_norm(x, g_pre_mix[i]) * (1.0 + scale) + shift
        if i % N_MIXERS == 0:
            y = dilated_attention(h, w_in_a[j], w_out_a[j])
        else:
            y = axial_gqa_attention(h, w_in_b[j], g_q_b[j], g_k_b[j], w_out_b[j])
        x = x + gate * rms_norm(y, g_post_mix[i])
        shift, scale, gate = modulation(c, w_mod[i, 1], b_mod[i, 1])
        h = rms_norm(x, g_pre_ffn[i]) * (1.0 + scale) + shift
        y = moe_ffn(h, w_router[i], b_router[i], w_up[i], b_up[i], w_down[i], b_down[i])
        x = x + gate * rms_norm(y, g_post_ffn[i])
    return x
```

```python
import functools

import jax
import jax.numpy as jnp
import numpy as np
from jax import lax
from jax.experimental import pallas as pl
from jax.experimental.pallas import tpu as pltpu

F32 = jnp.float32
BF16 = jnp.bfloat16
U32 = jnp.uint32
I32 = jnp.int32

D_MODEL = 1024
HEAD_DIM = 64
N_HEADS = 16
KV_HEADS = 4
GRID_W = 64
A_GROUPS = ((128, 1), (512, 4), (2048, 16))
ROPE_THETA = 10000.0
N_EXPERTS = 32
TOP_K = 4
D_FF = 1024
SWIGLU_LIMIT = 7.0
SWIGLU_ALPHA = 1.702
NORM_EPS = 1e-6
LANES = 128
NEG_BIG = -1e30

ROW_TILE = 1024
EXPERT_TILE = 256
TOKEN_TILE = 256
VMEM_LIMIT = 56 * 1024 * 1024


def _cparams(sem):
    return pltpu.CompilerParams(dimension_semantics=sem, vmem_limit_bytes=VMEM_LIMIT)


def _mod_kernel(c_ref, w_ref, b_ref, o_ref):
    c = c_ref[...]
    sc = c / (1.0 + jnp.exp(-c))
    o_ref[...] = jnp.dot(sc, w_ref[...], preferred_element_type=F32,
                         precision=lax.Precision.HIGHEST) + b_ref[...]


def _modulation(c, w_mod, b_mod):
    nb, d = c.shape
    n = w_mod.shape[0] * w_mod.shape[1]
    w = w_mod.reshape(n, d, 3 * d)
    b = b_mod.reshape(n, 1, 3 * d)
    out = pl.pallas_call(
        _mod_kernel,
        grid=(n, 3),
        in_specs=[pl.BlockSpec((nb, d), lambda i, j: (0, 0)),
                  pl.BlockSpec((None, d, d), lambda i, j: (i, 0, j)),
                  pl.BlockSpec((None, 1, d), lambda i, j: (i, 0, j))],
        out_specs=pl.BlockSpec((None, nb, d), lambda i, j: (i, 0, j)),
        out_shape=jax.ShapeDtypeStruct((n, nb, 3 * d), F32),
        compiler_params=_cparams(("parallel", "parallel")),
        name="modulation",
    )(c, w, b)
    return out.reshape(n, nb, 3, d)


def _norm_modulate(x, g, mod):
    ms = jnp.mean(x * x, axis=-1, keepdims=True)
    y = x * lax.rsqrt(ms + NORM_EPS) * g
    return y * (1.0 + mod[1:2, :]) + mod[0:1, :]


def _rotary(a, cos, sin_up, sin_dn, half):
    return (a * cos + pltpu.roll(a, LANES - half, 1) * sin_up + pltpu.roll(a, half, 1) * sin_dn)


def _rope_tables_1d(seq):
    half = HEAD_DIM // 2
    lane = np.arange(LANES)
    sub = lane % HEAD_DIM
    inv = 1.0 / (ROPE_THETA ** (jnp.arange(0, HEAD_DIM, 2, dtype=F32) / HEAD_DIM))
    ang = jnp.arange(seq, dtype=F32)[:, None] * inv[None, :]
    cos = jnp.cos(ang)[:, sub % half]
    sin = jnp.sin(ang)[:, sub % half]
    first = jnp.asarray(sub < half)[None, :]
    return cos, jnp.where(first, -sin, 0.0), jnp.where(first, 0.0, sin)


def _rope_tables_2d(seq):
    half = HEAD_DIM // 2
    quarter = half // 2
    lane = np.arange(LANES)
    sub = lane % HEAD_DIM
    inv = 1.0 / (ROPE_THETA ** (jnp.arange(0, half, 2, dtype=F32) / half))
    pos = jnp.arange(seq)
    row_ang = (pos // GRID_W).astype(F32)[:, None] * inv[None, :]
    col_ang = (pos % GRID_W).astype(F32)[:, None] * inv[None, :]
    f = sub % quarter
    is_row = jnp.asarray(sub < half)[None, :]
    cos = jnp.where(is_row, jnp.cos(row_ang)[:, f], jnp.cos(col_ang)[:, f])
    sin = jnp.where(is_row, jnp.sin(row_ang)[:, f], jnp.sin(col_ang)[:, f])
    first = jnp.asarray((sub % half) < quarter)[None, :]
    return cos, jnp.where(first, -sin, 0.0), jnp.where(first, 0.0, sin)


def _inproj_a_kernel(x_ref, mod_ref, g_ref, w_ref, cos_ref, sup_ref, sdn_ref, o_ref, h_ref):
    j = pl.program_id(1)

    @pl.when(j == 0)
    def _():
        h_ref[...] = _norm_modulate(x_ref[...], g_ref[...], mod_ref[...]).astype(BF16)

    acc = jnp.dot(h_ref[...], w_ref[...], preferred_element_type=F32)
    kind = j % 3

    @pl.when(kind == 2)
    def _():
        o_ref[...] = acc.astype(BF16)

    @pl.when(kind != 2)
    def _():
        scale = jnp.where(kind == 0, HEAD_DIM ** -0.5, 1.0).astype(F32)
        cos, sup, sdn = cos_ref[...] * scale, sup_ref[...] * scale, sdn_ref[...] * scale
        for c in range(D_MODEL // LANES):
            sl = slice(c * LANES, (c + 1) * LANES)
            o_ref[:, sl] = _rotary(acc[:, sl], cos, sup, sdn, HEAD_DIM // 2).astype(BF16)


def _inproj_a(x2, mod, g, w_bf, tables, seq):
    t, d = x2.shape
    n = w_bf.shape[1]
    tm = ROW_TILE
    per_b = seq // tm
    tab_spec = pl.BlockSpec((tm, LANES), lambda i, j: (i % per_b, 0))
    return pl.pallas_call(
        _inproj_a_kernel,
        grid=(t // tm, n // d),
        in_specs=[pl.BlockSpec((tm, d), lambda i, j: (i, 0)),
                  pl.BlockSpec((None, 3, d), lambda i, j: (i // per_b, 0, 0)),
                  pl.BlockSpec((1, d), lambda i, j: (0, 0)),
                  pl.BlockSpec((d, d), lambda i, j: (0, j)),
                  tab_spec, tab_spec, tab_spec],
        out_specs=pl.BlockSpec((tm, d), lambda i, j: (i, j)),
        out_shape=jax.ShapeDtypeStruct((t, n), BF16),
        scratch_shapes=[pltpu.VMEM((tm, d), BF16)],
        compiler_params=_cparams(("parallel", "arbitrary")),
        name="inproj_a",
    )(x2, mod, g, w_bf, *tables)


def _head_rms(a, g):
    r = lax.broadcasted_iota(I32, (LANES, LANES), 0) // HEAD_DIM
    c = lax.broadcasted_iota(I32, (LANES, LANES), 1) // HEAD_DIM
    ones_bd = (r == c).astype(F32)
    ss = jnp.dot(a * a, ones_bd, preferred_element_type=F32, precision=lax.Precision.HIGHEST)
    return a * lax.rsqrt(ss * (1.0 / HEAD_DIM) + NORM_EPS) * g


def _inproj_b_kernel(x_ref, mod_ref, g_ref, w_ref, gq_ref, gk_ref, cos_ref, sup_ref, sdn_ref,
                     o_ref, h_ref):
    j = pl.program_id(1)

    @pl.when(j == 0)
    def _():
        h_ref[...] = _norm_modulate(x_ref[...], g_ref[...], mod_ref[...]).astype(BF16)

    acc = jnp.dot(h_ref[...], w_ref[...], preferred_element_type=F32)
    cos, sup, sdn = cos_ref[...], sup_ref[...], sdn_ref[...]
    n_chunks = D_MODEL // LANES

    def normed(sl, g, scale):
        y = _rotary(_head_rms(acc[:, sl], g), cos, sup, sdn, HEAD_DIM // 4)
        return (y * scale).astype(BF16)

    @pl.when(j == 0)
    def _():
        for c in range(n_chunks):
            sl = slice(c * LANES, (c + 1) * LANES)
            o_ref[:, sl] = normed(sl, gq_ref[...], HEAD_DIM ** -0.5)

    @pl.when(j == 1)
    def _():
        for c in range(n_chunks):
            sl = slice(c * LANES, (c + 1) * LANES)
            if c < n_chunks // 2:
                o_ref[:, sl] = normed(sl, gk_ref[...], 1.0)
            else:
                o_ref[:, sl] = acc[:, sl].astype(BF16)


def _inproj_b(x2, mod, g, w_bf, gq, gk, tables, seq):
    t, d = x2.shape
    n = w_bf.shape[1]
    tm = ROW_TILE
    per_b = seq // tm
    tab_spec = pl.BlockSpec((tm, LANES), lambda i, j: (i % per_b, 0))
    vec_spec = pl.BlockSpec((1, LANES), lambda i, j: (0, 0))
    return pl.pallas_call(
        _inproj_b_kernel,
        grid=(t // tm, n // d),
        in_specs=[pl.BlockSpec((tm, d), lambda i, j: (i, 0)),
                  pl.BlockSpec((None, 3, d), lambda i, j: (i // per_b, 0, 0)),
                  pl.BlockSpec((1, d), lambda i, j: (0, 0)),
                  pl.BlockSpec((d, d), lambda i, j: (0, j)),
                  vec_spec, vec_spec, tab_spec, tab_spec, tab_spec],
        out_specs=pl.BlockSpec((tm, d), lambda i, j: (i, j)),
        out_shape=jax.ShapeDtypeStruct((t, n), BF16),
        scratch_shapes=[pltpu.VMEM((tm, d), BF16)],
        compiler_params=_cparams(("parallel", "arbitrary")),
        name="inproj_b",
    )(x2, mod, g, w_bf, gq, gk, *tables)


Q_TILE = 128


def _split_heads(qp):
    lane = lax.broadcasted_iota(I32, qp.shape, 1)
    zero = jnp.zeros_like(qp)
    return jnp.concatenate([jnp.where(lane < HEAD_DIM, qp, zero),
                            jnp.where(lane >= HEAD_DIM, qp, zero)], axis=0)


def _banded_kernel(q_ref, k_ref, v_ref, o_ref, lse_ref, *, seq_len, q_block, half):
    qi = pl.program_id(2)
    kw = min(2 * Q_TILE, seq_len)
    lane = lax.broadcasted_iota(I32, (Q_TILE, LANES), 1)
    col16 = lax.broadcasted_iota(I32, (Q_TILE, N_HEADS), 1)
    for t in range(q_block // Q_TILE):
        q0 = qi * q_block + t * Q_TILE
        kstart = jnp.clip(q0 - half, 0, seq_len - kw)
        kstart = pl.multiple_of(kstart, HEAD_DIM)
        qpos = q0 + lax.broadcasted_iota(I32, (Q_TILE, kw), 0)
        kpos = kstart + lax.broadcasted_iota(I32, (Q_TILE, kw), 1)
        band = jnp.abs(qpos - kpos) <= half
        band2 = jnp.concatenate([band, band], axis=0)
        rows = slice(t * Q_TILE, (t + 1) * Q_TILE)
        lse_acc = jnp.zeros((Q_TILE, N_HEADS), F32)
        for p in range(N_HEADS // 2):
            cols = slice(p * LANES, (p + 1) * LANES)
            q2 = _split_heads(q_ref[rows, cols])
            kp = k_ref[pl.ds(kstart, kw), cols]
            vp = v_ref[pl.ds(kstart, kw), cols]
            s = lax.dot_general(q2, kp, (((1,), (1,)), ((), ())), preferred_element_type=F32)
            s = jnp.where(band2, s, NEG_BIG)
            m = jnp.max(s, axis=-1, keepdims=True)
            e = jnp.exp(s - m)
            l = jnp.sum(e, axis=-1, keepdims=True)
            pv = jnp.dot(e.astype(BF16), vp, preferred_element_type=F32) / l
            o_ref[rows, cols] = jnp.where(lane < HEAD_DIM, pv[:Q_TILE], pv[Q_TILE:]).astype(BF16)
            lse = m + jnp.log(l)
            lse_acc = jnp.where(col16 == 2 * p, lse[:Q_TILE], lse_acc)
            lse_acc = jnp.where(col16 == 2 * p + 1, lse[Q_TILE:], lse_acc)
        lse_ref[rows, :] = lse_acc


def _banded_attention(proj, g, window, dil, batch, seq):
    d = D_MODEL
    n_cols = proj.shape[1]
    blocks = n_cols // d
    seq_len = seq // dil
    half = window // (2 * dil)
    q_block = min(seq_len, 512)
    pv = proj.reshape(batch, seq_len, dil * n_cols)
    kern = functools.partial(_banded_kernel, seq_len=seq_len, q_block=q_block, half=half)
    o, lse = pl.pallas_call(
        kern,
        grid=(batch, dil, seq_len // q_block),
        in_specs=[pl.BlockSpec((None, q_block, d), lambda b, r, i: (b, i, r * blocks + 3 * g)),
                  pl.BlockSpec((None, seq_len, d), lambda b, r, i: (b, 0, r * blocks + 3 * g + 1)),
                  pl.BlockSpec((None, seq_len, d), lambda b, r, i: (b, 0, r * blocks + 3 * g + 2))],
        out_specs=[pl.BlockSpec((None, q_block, d), lambda b, r, i: (b, i, r)),
                   pl.BlockSpec((None, None, q_block, N_HEADS), lambda b, r, i: (b, r, i, 0))],
        out_shape=[jax.ShapeDtypeStruct((batch, seq_len, dil * d), BF16),
                   jax.ShapeDtypeStruct((batch, dil, seq_len, N_HEADS), F32)],
        compiler_params=_cparams(("parallel", "parallel", "arbitrary")),
        name=f"banded_attn_g{g}",
    )(pv, pv, pv)
    o = o.reshape(batch * seq, d)
    lse = jnp.swapaxes(lse, 1, 2).reshape(batch * seq, N_HEADS)
    return o, lse


def _gqa_kernel(q_ref, k_ref, v_ref, o_ref):
    lane = lax.broadcasted_iota(I32, (Q_TILE, LANES), 1)
    grp = N_HEADS // KV_HEADS
    for kh in range(KV_HEADS):
        kcols = slice(kh * LANES, (kh + 1) * LANES)
        kd = k_ref[:, kcols]
        vd = v_ref[:, kcols]
        pairs = [q_ref[:, (kh * grp // 2 + i) * LANES:(kh * grp // 2 + i + 1) * LANES]
                 for i in range(grp // 2)]
        q4 = jnp.concatenate([_split_heads(qp) for qp in pairs], axis=0)
        s = lax.dot_general(q4, kd, (((1,), (1,)), ((), ())), preferred_element_type=F32)
        m = jnp.max(s, axis=-1, keepdims=True)
        e = jnp.exp(s - m)
        l = jnp.sum(e, axis=-1, keepdims=True)
        pv = jnp.dot(e.astype(BF16), vd, preferred_element_type=F32) / l
        for i in range(grp // 2):
            lo = pv[(2 * i) * Q_TILE:(2 * i + 1) * Q_TILE]
            hi = pv[(2 * i + 1) * Q_TILE:(2 * i + 2) * Q_TILE]
            c0 = (kh * grp // 2 + i) * LANES
            o_ref[:, c0:c0 + LANES] = jnp.where(lane < HEAD_DIM, lo, hi).astype(BF16)


def _gqa_attention(proj, batch, seq):
    d = D_MODEL
    kvw = KV_HEADS * LANES
    p3 = proj.reshape(batch, seq, proj.shape[1])
    o = pl.pallas_call(
        _gqa_kernel,
        grid=(batch, seq // Q_TILE),
        in_specs=[pl.BlockSpec((None, Q_TILE, d), lambda b, i: (b, i, 0)),
                  pl.BlockSpec((None, seq, kvw), lambda b, i: (b, 0, d // kvw)),
                  pl.BlockSpec((None, seq, kvw), lambda b, i: (b, 0, d // kvw + 1))],
        out_specs=pl.BlockSpec((None, Q_TILE, d), lambda b, i: (b, i, 0)),
        out_shape=jax.ShapeDtypeStruct((batch, seq, d), BF16),
        compiler_params=_cparams(("parallel", "arbitrary")),
        name="gqa_attn",
    )(p3, p3, p3)
    return o.reshape(batch * seq, d)


def _post_residual(x, y, g, mod):
    ms = jnp.mean(y * y, axis=-1, keepdims=True)
    return x + mod[2:3, :] * (y * lax.rsqrt(ms + NORM_EPS) * g)


def _outproj_kernel(*refs, n_groups):
    o_refs = refs[:n_groups]
    rest = refs[n_groups:]
    if n_groups > 1:
        lse_ref, rest = rest[0], rest[1:]
    x_ref, mod_ref, g_ref, w_ref, out_ref = rest
    if n_groups > 1:
        lse = [lse_ref[gi] for gi in range(n_groups)]
        m = functools.reduce(jnp.maximum, lse)
        ex = [jnp.exp(v - m) for v in lse]
        den = functools.reduce(lambda a, b: a + b, ex)
        hr = lax.broadcasted_iota(I32, (N_HEADS, D_MODEL), 0)
        hc = lax.broadcasted_iota(I32, (N_HEADS, D_MODEL), 1) // HEAD_DIM
        expand = (hr == hc).astype(F32)
        o = None
        for gi in range(n_groups):
            w = jnp.dot(ex[gi] / den, expand, preferred_element_type=F32,
                        precision=lax.Precision.HIGHEST)
            term = w * o_refs[gi][...].astype(F32)
            o = term if o is None else o + term
        o = o.astype(BF16)
    else:
        o = o_refs[0][...]
    y = jnp.dot(o, w_ref[...], preferred_element_type=F32)
    out_ref[...] = _post_residual(x_ref[...], y, g_ref[...], mod_ref[...])


def _outproj(os_, lse, x2, mod, g, w_bf, seq):
    t, d = x2.shape
    tm = 512
    per_b = seq // tm
    n_groups = len(os_)
    row_spec = pl.BlockSpec((tm, d), lambda i: (i, 0))
    in_specs = [row_spec] * n_groups
    args = list(os_)
    if n_groups > 1:
        in_specs.append(pl.BlockSpec((n_groups, tm, N_HEADS), lambda i: (0, i, 0)))
        args.append(lse)
    in_specs += [row_spec,
                 pl.BlockSpec((None, 3, d), lambda i: (i // per_b, 0, 0)),
                 pl.BlockSpec((1, d), lambda i: (0, 0)),
                 pl.BlockSpec((d, d), lambda i: (0, 0))]
    args += [x2, mod, g, w_bf]
    return pl.pallas_call(
        functools.partial(_outproj_kernel, n_groups=n_groups),
        grid=(t // tm,),
        in_specs=in_specs,
        out_specs=row_spec,
        out_shape=jax.ShapeDtypeStruct((t, d), F32),
        compiler_params=_cparams(("parallel",)),
        name=f"outproj_{n_groups}",
    )(*args)


def _pack_bf16_pairs(h):
    half = h.shape[1] // 2
    bits = pltpu.bitcast(h.astype(BF16).astype(F32), U32)
    return (bits[:, :half] >> 16) | (bits[:, half:] & jnp.uint32(0xFFFF0000))


def _unpack_bf16_pairs(p):
    lo = pltpu.bitcast(p << 16, F32)
    hi = pltpu.bitcast(p & jnp.uint32(0xFFFF0000), F32)
    return lo, hi


def _router_kernel(x_ref, mod_ref, g_ref, wr_ref, br_ref, hp_ref, gate_ref, idx_ref):
    h = _norm_modulate(x_ref[...], g_ref[...], mod_ref[...])
    hp_ref[...] = _pack_bf16_pairs(h)
    logits = jnp.dot(h, wr_ref[...], preferred_element_type=F32,
                     precision=lax.Precision.HIGHEST) + br_ref[...]
    tm = logits.shape[0]
    lane = lax.broadcasted_iota(I32, (tm, N_EXPERTS), 1)
    kcol = lax.broadcasted_iota(I32, (tm, TOP_K), 1)
    vals = jnp.zeros((tm, TOP_K), F32)
    idxs = jnp.zeros((tm, TOP_K), I32)
    cur = logits
    for k in range(TOP_K):
        m = jnp.max(cur, axis=-1, keepdims=True)
        i = jnp.min(jnp.where(cur == m, lane, N_EXPERTS), axis=-1, keepdims=True)
        vals = jnp.where(kcol == k, m, vals)
        idxs = jnp.where(kcol == k, i, idxs)
        cur = jnp.where(lane == i, -jnp.inf, cur)
    e = jnp.exp(vals - vals[:, 0:1])
    gate_ref[...] = e / jnp.sum(e, axis=-1, keepdims=True)
    idx_ref[...] = idxs


def _router(x2, mod, g, w_router, b_router, seq):
    t, d = x2.shape
    tm = 512
    per_b = seq // tm
    return pl.pallas_call(
        _router_kernel,
        grid=(t // tm,),
        in_specs=[pl.BlockSpec((tm, d), lambda i: (i, 0)),
                  pl.BlockSpec((None, 3, d), lambda i: (i // per_b, 0, 0)),
                  pl.BlockSpec((1, d), lambda i: (0, 0)),
                  pl.BlockSpec((d, N_EXPERTS), lambda i: (0, 0)),
                  pl.BlockSpec((1, N_EXPERTS), lambda i: (0, 0))],
        out_specs=[pl.BlockSpec((tm, d // 2), lambda i: (i, 0)),
                   pl.BlockSpec((tm, TOP_K), lambda i: (i, 0)),
                   pl.BlockSpec((tm, TOP_K), lambda i: (i, 0))],
        out_shape=[jax.ShapeDtypeStruct((t, d // 2), U32),
                   jax.ShapeDtypeStruct((t, TOP_K), F32),
                   jax.ShapeDtypeStruct((t, TOP_K), I32)],
        compiler_params=_cparams(("parallel",)),
        name="router",
    )(x2, mod, g, w_router, b_router)


def _rank_kernel(idx_ref, pos_ref, starts_ref, cnt_ref, carry_ref):
    ph = pl.program_id(0)
    i = pl.program_id(1)
    tm = idx_ref.shape[0]
    lane = lax.broadcasted_iota(I32, (tm, N_EXPERTS), 1)
    idx = idx_ref[...]
    onehot = [(lane == idx[:, k:k+1]).astype(F32) for k in range(TOP_K)]
    cnt = functools.reduce(lambda a, b: a + b, onehot)

    @pl.when((ph == 0) & (i == 0))
    def _():
        carry_ref[...] = jnp.zeros_like(carry_ref)

    @pl.when(ph == 0)
    def _():
        carry_ref[...] += jnp.sum(cnt, axis=0, keepdims=True)

    @pl.when((ph == 1) & (i == 0))
    def _():
        total = carry_ref[...]
        cnt_ref[...] = total
        padded = jnp.ceil(total * (1.0 / EXPERT_TILE)) * EXPERT_TILE
        r = lax.broadcasted_iota(I32, (N_EXPERTS, N_EXPERTS), 0)
        c = lax.broadcasted_iota(I32, (N_EXPERTS, N_EXPERTS), 1)
        starts = jnp.dot(jnp.broadcast_to(padded, (8, N_EXPERTS)), jnp.where(r < c, 1.0, 0.0),
                         preferred_element_type=F32,
                         precision=lax.Precision.HIGHEST)[0:1]
        starts_ref[...] = starts
        carry_ref[...] = starts

    @pl.when(ph == 1)
    def _():
        r = lax.broadcasted_iota(I32, (tm, tm), 0)
        c = lax.broadcasted_iota(I32, (tm, tm), 1)
        before = jnp.dot(jnp.where(c < r, 1.0, 0.0).astype(BF16), cnt.astype(BF16),
                         preferred_element_type=F32)
        base = carry_ref[...] + before
        kcol = lax.broadcasted_iota(I32, (tm, TOP_K), 1)
        pos = jnp.zeros((tm, TOP_K), F32)
        for k in range(TOP_K):
            pos = jnp.where(kcol == k, jnp.sum(onehot[k] * base, axis=-1, keepdims=True), pos)
        pos_ref[...] = pos.astype(I32)
        carry_ref[...] += jnp.sum(cnt, axis=0, keepdims=True)


def _rank(idx):
    t = idx.shape[0]
    tm = 512
    vec = pl.BlockSpec((1, N_EXPERTS), lambda ph, i: (0, 0))
    return pl.pallas_call(
        _rank_kernel,
        grid=(2, t // tm),
        in_specs=[pl.BlockSpec((tm, TOP_K), lambda ph, i: (i, 0))],
        out_specs=[pl.BlockSpec((tm, TOP_K), lambda ph, i: (i * ph, 0)), vec, vec],
        out_shape=[jax.ShapeDtypeStruct((t, TOP_K), I32),
                   jax.ShapeDtypeStruct((1, N_EXPERTS), F32),
                   jax.ShapeDtypeStruct((1, N_EXPERTS), F32)],
        scratch_shapes=[pltpu.VMEM((1, N_EXPERTS), F32)],
        compiler_params=_cparams(("arbitrary", "arbitrary")),
        name="rank",
    )(idx)


def _row_copy(src, s, dst, d, sem):
    return pltpu.make_async_copy(src.at[pl.ds(s, 1)], dst.at[pl.ds(d, 1)], sem)


def _dispatch_kernel(pos_ref, h_ref, xs_in_ref, xs_ref, sem):
    del xs_in_ref
    base = pl.program_id(0) * TOKEN_TILE
    n = TOKEN_TILE * TOP_K

    def issue(t, carry):
        for k in range(TOP_K):
            _row_copy(h_ref, base + t, xs_ref, pos_ref[t * TOP_K + k], sem).start()
        return carry

    lax.fori_loop(0, TOKEN_TILE, issue, 0)
    pltpu.make_async_copy(h_ref.at[pl.ds(0, n)], xs_ref.at[pl.ds(0, n)], sem).wait()


def _dispatch(pos_flat, hp, n_rows):
    t, w = hp.shape
    n = TOKEN_TILE * TOP_K
    xs0 = jnp.zeros((n_rows, w), U32)
    return pl.pallas_call(
        _dispatch_kernel,
        grid=(t // TOKEN_TILE,),
        in_specs=[pl.BlockSpec((n,), lambda i: (i,), memory_space=pltpu.SMEM),
                  pl.BlockSpec(memory_space=pl.ANY),
                  pl.BlockSpec(memory_space=pl.ANY)],
        out_specs=pl.BlockSpec(memory_space=pl.ANY),
        out_shape=jax.ShapeDtypeStruct((n_rows, w), U32),
        scratch_shapes=[pltpu.SemaphoreType.DMA(())],
        input_output_aliases={2: 0},
        compiler_params=_cparams(("arbitrary",)),
        name="dispatch",
    )(pos_flat, hp, xs0)


def _expert_kernel(te_ref, nv_ref, xs_ref, wu_ref, bu_ref, wd_ref, bd_ref, ys_ref, wu_bf, wd_bf):
    i = pl.program_id(0)
    prev = te_ref[jnp.maximum(i - 1, 0)]
    valid = i < nv_ref[0]

    @pl.when(valid & ((i == 0) | (te_ref[i] != prev)))
    def _():
        wu_bf[...] = wu_ref[...].astype(BF16)
        wd_bf[...] = wd_ref[...].astype(BF16)

    @pl.when(valid)
    def _():
        lo, hi = _unpack_bf16_pairs(xs_ref[...])
        half = D_MODEL // 2
        hu = (jnp.dot(lo.astype(BF16), wu_bf[:half, :], preferred_element_type=F32)
              + jnp.dot(hi.astype(BF16), wu_bf[half:, :], preferred_element_type=F32)
              + bu_ref[...])
        x_glu = jnp.minimum(hu[:, :D_FF], SWIGLU_LIMIT)
        x_lin = jnp.clip(hu[:, D_FF:], -SWIGLU_LIMIT, SWIGLU_LIMIT)
        act = x_glu / (1.0 + jnp.exp(-SWIGLU_ALPHA * x_glu)) * (x_lin + 1.0)
        y = jnp.dot(act.astype(BF16), wd_bf[...], preferred_element_type=F32) + bd_ref[...]
        ys_ref[...] = _pack_bf16_pairs(y)

    @pl.when(jnp.logical_not(valid))
    def _():
        ys_ref[...] = jnp.zeros_like(ys_ref)


def _experts(tile_expert, n_valid, xs, w_up, b_up, w_down, b_down):
    n_rows, w = xs.shape
    d = D_MODEL
    n_tiles = n_rows // EXPERT_TILE
    grid_spec = pltpu.PrefetchScalarGridSpec(
        num_scalar_prefetch=2,
        grid=(n_tiles,),
        in_specs=[pl.BlockSpec((EXPERT_TILE, w), lambda i, te, nv: (i, 0)),
                  pl.BlockSpec((None, d, 2 * D_FF), lambda i, te, nv: (te[i], 0, 0)),
                  pl.BlockSpec((None, 1, 2 * D_FF), lambda i, te, nv: (te[i], 0, 0)),
                  pl.BlockSpec((None, D_FF, d), lambda i, te, nv: (te[i], 0, 0)),
                  pl.BlockSpec((None, 1, d), lambda i, te, nv: (te[i], 0, 0))],
        out_specs=pl.BlockSpec((EXPERT_TILE, w), lambda i, te, nv: (i, 0)),
        scratch_shapes=[pltpu.VMEM((d, 2 * D_FF), BF16), pltpu.VMEM((D_FF, d), BF16)],
    )
    return pl.pallas_call(
        _expert_kernel,
        grid_spec=grid_spec,
        out_shape=jax.ShapeDtypeStruct((n_rows, w), U32),
        compiler_params=_cparams(("arbitrary",)),
        name="experts",
    )(tile_expert, n_valid, xs, w_up, b_up.reshape(N_EXPERTS, 1, -1), w_down,
      b_down.reshape(N_EXPERTS, 1, -1))


def _combine_kernel(pos_ref, ys_ref, gate_ref, x_ref, mod_ref, g_ref, out_ref, buf, sem):
    def issue(t, carry):
        for k in range(TOP_K):
            _row_copy(ys_ref, pos_ref[t * TOP_K + k], buf.at[k], t, sem).start()
        return carry

    lax.fori_loop(0, TOKEN_TILE, issue, 0)
    for k in range(TOP_K):
        pltpu.make_async_copy(ys_ref.at[pl.ds(0, TOKEN_TILE)], buf.at[k], sem).wait()

    gates = gate_ref[...]
    y_lo = y_hi = None
    for k in range(TOP_K):
        lo, hi = _unpack_bf16_pairs(buf[k])
        gk = gates[:, k:k+1]
        y_lo = gk * lo if y_lo is None else y_lo + gk * lo
        y_hi = gk * hi if y_hi is None else y_hi + gk * hi
    y = jnp.concatenate([y_lo, y_hi], axis=-1)
    out_ref[...] = _post_residual(x_ref[...], y, g_ref[...], mod_ref[...])


def _combine(pos_flat, ys, gates, x2, mod, g, seq):
    t, d = x2.shape
    tm = TOKEN_TILE
    per_b = seq // tm
    n = tm * TOP_K
    return pl.pallas_call(
        _combine_kernel,
        grid=(t // tm,),
        in_specs=[pl.BlockSpec((n,), lambda i: (i,), memory_space=pltpu.SMEM),
                  pl.BlockSpec(memory_space=pl.ANY),
                  pl.BlockSpec((tm, TOP_K), lambda i: (i, 0)),
                  pl.BlockSpec((tm, d), lambda i: (i, 0)),
                  pl.BlockSpec((None, 3, d), lambda i: (i // per_b, 0, 0)),
                  pl.BlockSpec((1, d), lambda i: (0, 0))],
        out_specs=pl.BlockSpec((tm, d), lambda i: (i, 0)),
        out_shape=jax.ShapeDtypeStruct((t, d), F32),
        scratch_shapes=[pltpu.VMEM((TOP_K, tm, d // 2), U32), pltpu.SemaphoreType.DMA(())],
        compiler_params=_cparams(("arbitrary",)),
        name="combine",
    )(pos_flat, ys, gates, x2, mod, g)


def _moe(x2, mod, g_pre, g_post, w_router, b_router, w_up, b_up, w_down, b_down, seq):
    t = x2.shape[0]
    hp, gates, idx = _router(x2, mod, g_pre, w_router, b_router.reshape(1, -1), seq)
    pos, _, counts = _rank(idx)
    n_pairs = t * TOP_K
    n_tiles = n_pairs // EXPERT_TILE + N_EXPERTS
    n_rows = n_tiles * EXPERT_TILE
    tiles_per_e = (counts[0].astype(I32) + EXPERT_TILE - 1) // EXPERT_TILE
    tile_end = jnp.cumsum(tiles_per_e)
    n_valid = tile_end[-1:]
    tile_expert = jnp.sum(jnp.arange(n_tiles, dtype=I32)[:, None] >= tile_end[None, :], axis=1).astype(I32)
    last = jnp.sum(n_valid[0] - 1 >= tile_end).astype(I32)
    tile_expert = jnp.minimum(tile_expert, last)
    pos_flat = pos.reshape(n_pairs)
    xs = _dispatch(pos_flat, hp, n_rows)
    ys = _experts(tile_expert, n_valid.astype(I32), xs, w_up, b_up, w_down, b_down)
    return _combine(pos_flat, ys, gates, x2, mod, g_post, seq)


def _dup_heads(w, n_heads):
    dm = w.shape[0]
    w3 = w.reshape(dm, n_heads, 1, HEAD_DIM)
    return jnp.broadcast_to(w3, (dm, n_heads, 2, HEAD_DIM)).reshape(dm, n_heads * 2 * HEAD_DIM)


def kernel(x, c, w_mod, b_mod, g_pre_mix, g_post_mix, g_pre_ffn, g_post_ffn, w_in_a, w_out_a,
           w_in_b, g_q_b, g_k_b, w_out_b, w_router, b_router, w_up, b_up, w_down, b_down):
    batch, seq, d = x.shape
    depth = w_mod.shape[0]
    x2 = x.reshape(batch * seq, d)
    mods = _modulation(c, w_mod, b_mod)
    tab1 = _rope_tables_1d(seq)
    tab2 = _rope_tables_2d(seq)
    for i in range(depth):
        j = i // 2
        mod_mix, mod_ffn = mods[2 * i], mods[2 * i + 1]
        g_pre = g_pre_mix[i].reshape(1, d)
        g_post = g_post_mix[i].reshape(1, d)
        if i % 2 == 0:
            proj = _inproj_a(x2, mod_mix, g_pre, w_in_a[j].astype(BF16), tab1, seq)
            outs, lses = [], []
            for g, (window, dil) in enumerate(A_GROUPS):
                o, lse = _banded_attention(proj, g, window, dil, batch, seq)
                outs.append(o)
                lses.append(lse)
            x2 = _outproj(outs, jnp.stack(lses, 0), x2, mod_mix, g_post, w_out_a[j].astype(BF16), seq)
        else:
            qd, kd = N_HEADS * HEAD_DIM, KV_HEADS * HEAD_DIM
            wb = w_in_b[j]
            w_cat = jnp.concatenate([wb[:, :qd], _dup_heads(wb[:, qd:qd + kd], KV_HEADS),
                                     _dup_heads(wb[:, qd + kd:], KV_HEADS)], axis=1).astype(BF16)
            gq = jnp.tile(g_q_b[j], LANES // HEAD_DIM).reshape(1, LANES)
            gk = jnp.tile(g_k_b[j], LANES // HEAD_DIM).reshape(1, LANES)
            proj = _inproj_b(x2, mod_mix, g_pre, w_cat, gq, gk, tab2, seq)
            o = _gqa_attention(proj, batch, seq)
            x2 = _outproj([o], None, x2, mod_mix, g_post, w_out_b[j].astype(BF16), seq)
        x2 = _moe(x2, mod_ffn, g_pre_ffn[i].reshape(1, d), g_post_ffn[i].reshape(1, d),
                  w_router[i], b_router[i], w_up[i], b_up[i], w_down[i], b_down[i], seq)
    return x2.reshape(batch, seq, d)
```

```python
import functools

import jax
import jax.numpy as jnp
import numpy as np
from jax import lax
from jax.experimental import pallas as pl
from jax.experimental.pallas import tpu as pltpu

F32 = jnp.float32
BF16 = jnp.bfloat16
I32 = jnp.int32

D_MODEL = 1024
HEAD_DIM = 64
N_HEADS = 16
KV_HEADS = 4
GRID_W = 64
A_GROUPS = ((128, 1), (512, 4), (2048, 16))
ROPE_THETA = 10000.0
N_EXPERTS = 32
TOP_K = 4
D_FF = 1024
SWIGLU_LIMIT = 7.0
SWIGLU_ALPHA = 1.702
NORM_EPS = 1e-6
LANES = 128
NEG_BIG = -1e30

ROW_TILE = 1024
EXPERT_TILE = 256
TOKEN_TILE = 256
Q_TILE = 128
HEAD_BLOCK = 256
VMEM_LIMIT = 56 * 1024 * 1024


def _cparams(sem):
    return pltpu.CompilerParams(dimension_semantics=sem, vmem_limit_bytes=VMEM_LIMIT)


def _mod_kernel(c_ref, w_ref, b_ref, o_ref):
    c = c_ref[...]
    sc = c / (1.0 + jnp.exp(-c))
    o_ref[...] = jnp.dot(sc, w_ref[...], preferred_element_type=F32,
                         precision=lax.Precision.HIGHEST) + b_ref[...]


def _modulation(c, w_mod, b_mod):
    nb, d = c.shape
    n = w_mod.shape[0] * w_mod.shape[1]
    w = w_mod.reshape(n, d, 3 * d)
    b = b_mod.reshape(n, 1, 3 * d)
    out = pl.pallas_call(
        _mod_kernel,
        grid=(n, 3),
        in_specs=[pl.BlockSpec((nb, d), lambda i, j: (0, 0)),
                  pl.BlockSpec((None, d, d), lambda i, j: (i, 0, j)),
                  pl.BlockSpec((None, 1, d), lambda i, j: (i, 0, j))],
        out_specs=pl.BlockSpec((None, nb, d), lambda i, j: (i, 0, j)),
        out_shape=jax.ShapeDtypeStruct((n, nb, 3 * d), F32),
        compiler_params=_cparams(("parallel", "parallel")),
        name="modulation",
    )(c, w, b)
    return out.reshape(n, nb, 3, d)


def _norm_modulate(x, g, mod):
    ms = jnp.mean(x * x, axis=-1, keepdims=True)
    y = x * lax.rsqrt(ms + NORM_EPS) * g
    return y * (1.0 + mod[1:2, :]) + mod[0:1, :]


def _rotary(a, cos, sin_up, sin_dn, half):
    return (a * cos + pltpu.roll(a, LANES - half, 1) * sin_up + pltpu.roll(a, half, 1) * sin_dn)


def _rope_tables_1d(seq):
    half = HEAD_DIM // 2
    lane = np.arange(LANES)
    sub = lane % HEAD_DIM
    inv = 1.0 / (ROPE_THETA ** (jnp.arange(0, HEAD_DIM, 2, dtype=F32) / HEAD_DIM))
    ang = jnp.arange(seq, dtype=F32)[:, None] * inv[None, :]
    cos = jnp.cos(ang)[:, sub % half]
    sin = jnp.sin(ang)[:, sub % half]
    first = jnp.asarray(sub < half)[None, :]
    return cos, jnp.where(first, -sin, 0.0), jnp.where(first, 0.0, sin)


def _rope_tables_2d(seq):
    half = HEAD_DIM // 2
    quarter = half // 2
    lane = np.arange(LANES)
    sub = lane % HEAD_DIM
    inv = 1.0 / (ROPE_THETA ** (jnp.arange(0, half, 2, dtype=F32) / half))
    pos = jnp.arange(seq)
    row_ang = (pos // GRID_W).astype(F32)[:, None] * inv[None, :]
    col_ang = (pos % GRID_W).astype(F32)[:, None] * inv[None, :]
    f = sub % quarter
    is_row = jnp.asarray(sub < half)[None, :]
    cos = jnp.where(is_row, jnp.cos(row_ang)[:, f], jnp.cos(col_ang)[:, f])
    sin = jnp.where(is_row, jnp.sin(row_ang)[:, f], jnp.sin(col_ang)[:, f])
    first = jnp.asarray((sub % half) < quarter)[None, :]
    return cos, jnp.where(first, -sin, 0.0), jnp.where(first, 0.0, sin)


def _residue_major(table, tile, dil):
    seq, w = table.shape
    n = tile // dil
    return table.reshape(seq // tile, n, dil, w).swapaxes(1, 2).reshape(seq, w)


def _inproj_a_kernel(x_ref, mod_ref, g_ref, w_ref, cos_ref, sup_ref, sdn_ref, o_ref, h_ref, *scratch,
                     dil):
    j = pl.program_id(1)
    tm = x_ref.shape[0]
    n = tm // dil

    @pl.when(j == 0)
    def _():
        h = _norm_modulate(x_ref[...], g_ref[...], mod_ref[...])
        if dil == 1:
            h_ref[...] = h.astype(BF16)
        else:
            (stage,) = scratch
            for c in range(D_MODEL // LANES):
                stage[c] = h[:, c * LANES:(c + 1) * LANES]
            for c in range(D_MODEL // LANES):
                for r in range(dil):
                    h_ref[r * n:(r + 1) * n, c * LANES:(c + 1) * LANES] = (
                        stage[c, pl.ds(r, n, stride=dil), :].astype(BF16))

    acc = jnp.dot(h_ref[...], w_ref[...], preferred_element_type=F32)

    def store(sl, val):
        for r in range(dil):
            o_ref[r, :, sl] = val[r * n:(r + 1) * n]

    @pl.when(j == 2)
    def _():
        store(slice(None), acc.astype(BF16))

    @pl.when(j != 2)
    def _():
        scale = jnp.where(j == 0, HEAD_DIM ** -0.5, 1.0).astype(F32)
        cos, sup, sdn = cos_ref[...] * scale, sup_ref[...] * scale, sdn_ref[...] * scale
        for c in range(D_MODEL // LANES):
            sl = slice(c * LANES, (c + 1) * LANES)
            store(sl, _rotary(acc[:, sl], cos, sup, sdn, HEAD_DIM // 2).astype(BF16))


def _inproj_a(x2, mod, g, w_bf, tables, batch, seq, group, dil):
    t, d = x2.shape
    tm = ROW_TILE
    per_b = seq // tm
    n = tm // dil
    tabs = [_residue_major(tb, tm, dil) for tb in tables]
    tab_spec = pl.BlockSpec((tm, LANES), lambda i, j: (i % per_b, 0))
    return pl.pallas_call(
        functools.partial(_inproj_a_kernel, dil=dil),
        grid=(t // tm, 3),
        in_specs=[pl.BlockSpec((tm, d), lambda i, j: (i, 0)),
                  pl.BlockSpec((None, 3, d), lambda i, j: (i // per_b, 0, 0)),
                  pl.BlockSpec((1, d), lambda i, j: (0, 0)),
                  pl.BlockSpec((d, d), lambda i, j: (0, 3 * group + j)),
                  tab_spec, tab_spec, tab_spec],
        out_specs=pl.BlockSpec((None, dil, n, d), lambda i, j: (i // per_b, 0, i % per_b, j)),
        out_shape=jax.ShapeDtypeStruct((batch, dil, seq // dil, 3 * d), BF16),
        scratch_shapes=[pltpu.VMEM((tm, d), BF16)]
        + ([] if dil == 1 else [pltpu.VMEM((d // LANES, tm, LANES), F32)]),
        compiler_params=_cparams(("parallel", "arbitrary")),
        name=f"inproj_a_g{group}",
    )(x2, mod, g, w_bf, *tabs)


def _head_rms(a, g):
    r = lax.broadcasted_iota(I32, (LANES, LANES), 0) // HEAD_DIM
    c = lax.broadcasted_iota(I32, (LANES, LANES), 1) // HEAD_DIM
    ones_bd = (r == c).astype(F32)
    ss = jnp.dot(a * a, ones_bd, preferred_element_type=F32, precision=lax.Precision.HIGHEST)
    return a * lax.rsqrt(ss * (1.0 / HEAD_DIM) + NORM_EPS) * g


def _inproj_b_kernel(x_ref, mod_ref, g_ref, w_ref, gq_ref, gk_ref, cos_ref, sup_ref, sdn_ref,
                     o_ref, h_ref):
    j = pl.program_id(1)

    @pl.when(j == 0)
    def _():
        h_ref[...] = _norm_modulate(x_ref[...], g_ref[...], mod_ref[...]).astype(BF16)

    acc = jnp.dot(h_ref[...], w_ref[...], preferred_element_type=F32)
    cos, sup, sdn = cos_ref[...], sup_ref[...], sdn_ref[...]
    n_chunks = D_MODEL // LANES

    def normed(sl, g, scale):
        y = _rotary(_head_rms(acc[:, sl], g), cos, sup, sdn, HEAD_DIM // 4)
        return (y * scale).astype(BF16)

    @pl.when(j == 0)
    def _():
        for c in range(n_chunks):
            sl = slice(c * LANES, (c + 1) * LANES)
            o_ref[:, sl] = normed(sl, gq_ref[...], HEAD_DIM ** -0.5)

    @pl.when(j == 1)
    def _():
        for c in range(n_chunks):
            sl = slice(c * LANES, (c + 1) * LANES)
            if c < n_chunks // 2:
                o_ref[:, sl] = normed(sl, gk_ref[...], 1.0)
            else:
                o_ref[:, sl] = acc[:, sl].astype(BF16)


def _inproj_b(x2, mod, g, w_bf, gq, gk, tables, seq):
    t, d = x2.shape
    n = w_bf.shape[1]
    tm = ROW_TILE
    per_b = seq // tm
    tab_spec = pl.BlockSpec((tm, LANES), lambda i, j: (i % per_b, 0))
    vec_spec = pl.BlockSpec((1, LANES), lambda i, j: (0, 0))
    return pl.pallas_call(
        _inproj_b_kernel,
        grid=(t // tm, n // d),
        in_specs=[pl.BlockSpec((tm, d), lambda i, j: (i, 0)),
                  pl.BlockSpec((None, 3, d), lambda i, j: (i // per_b, 0, 0)),
                  pl.BlockSpec((1, d), lambda i, j: (0, 0)),
                  pl.BlockSpec((d, d), lambda i, j: (0, j)),
                  vec_spec, vec_spec, tab_spec, tab_spec, tab_spec],
        out_specs=pl.BlockSpec((tm, d), lambda i, j: (i, j)),
        out_shape=jax.ShapeDtypeStruct((t, n), BF16),
        scratch_shapes=[pltpu.VMEM((tm, d), BF16)],
        compiler_params=_cparams(("parallel", "arbitrary")),
        name="inproj_b",
    )(x2, mod, g, w_bf, gq, gk, *tables)


def _split_heads(qp):
    lane = lax.broadcasted_iota(I32, qp.shape, 1)
    zero = jnp.zeros_like(qp)
    return jnp.concatenate([jnp.where(lane < HEAD_DIM, qp, zero),
                            jnp.where(lane >= HEAD_DIM, qp, zero)], axis=0)


def _banded_kernel(q_ref, k_ref, v_ref, o_ref, lse_ref, *scratch, dil, seq_len, q_block, half):
    qi = pl.program_id(2)
    kw = min(2 * Q_TILE, seq_len)
    heads = HEAD_BLOCK // HEAD_DIM
    lane = lax.broadcasted_iota(I32, (Q_TILE, LANES), 1)
    hcol = lax.broadcasted_iota(I32, (Q_TILE, LANES), 1)
    if dil > 1:
        o_acc, lse_acc = scratch
    for r in range(dil):
        for t in range(q_block // Q_TILE):
            q0 = qi * q_block + t * Q_TILE
            kstart = pl.multiple_of(jnp.clip(q0 - half, 0, seq_len - kw), HEAD_DIM)
            qpos = q0 + lax.broadcasted_iota(I32, (Q_TILE, kw), 0)
            kpos = kstart + lax.broadcasted_iota(I32, (Q_TILE, kw), 1)
            band = jnp.abs(qpos - kpos) <= half
            band2 = jnp.concatenate([band, band], axis=0)
            if dil == 1:
                rows = slice(t * Q_TILE, (t + 1) * Q_TILE)
            else:
                rows = pl.ds(t * Q_TILE * dil + r, Q_TILE, stride=dil)
            lse_t = jnp.zeros((Q_TILE, LANES), F32)
            for p in range(HEAD_BLOCK // LANES):
                cols = slice(p * LANES, (p + 1) * LANES)
                q2 = _split_heads(q_ref[r, t * Q_TILE:(t + 1) * Q_TILE, cols])
                kp = k_ref[r, pl.ds(kstart, kw), cols]
                vp = v_ref[r, pl.ds(kstart, kw), cols]
                s = lax.dot_general(q2, kp, (((1,), (1,)), ((), ())), preferred_element_type=F32)
                s = jnp.where(band2, s, NEG_BIG)
                m = jnp.max(s, axis=-1, keepdims=True)
                e = jnp.exp(s - m)
                l = jnp.sum(e, axis=-1, keepdims=True)
                pv = jnp.dot(e.astype(BF16), vp, preferred_element_type=F32) / l
                o = jnp.where(lane < HEAD_DIM, pv[:Q_TILE], pv[Q_TILE:])
                if dil == 1:
                    o_ref[rows, cols] = o.astype(BF16)
                else:
                    o_acc[p, rows, :] = o
                lse = m + jnp.log(l)
                lse_t = jnp.where(hcol == 2 * p, lse[:Q_TILE], lse_t)
                lse_t = jnp.where(hcol == 2 * p + 1, lse[Q_TILE:], lse_t)
            if dil == 1:
                lse_ref[rows, :] = lse_t[:, :heads]
            else:
                lse_acc[rows, :] = lse_t
    if dil > 1:
        for p in range(HEAD_BLOCK // LANES):
            o_ref[:, p * LANES:(p + 1) * LANES] = o_acc[p].astype(BF16)
        lse_ref[...] = lse_acc[:, :heads]


def _banded_attention(qkv, window, dil, batch, seq, group):
    d = D_MODEL
    seq_len = seq // dil
    half = window // (2 * dil)
    q_block = max(Q_TILE, min(seq_len, 512 // dil))
    rows = q_block * dil
    n_hb = d // HEAD_BLOCK
    heads = HEAD_BLOCK // HEAD_DIM
    kern = functools.partial(_banded_kernel, dil=dil, seq_len=seq_len, q_block=q_block, half=half)
    scratch = [] if dil == 1 else [pltpu.VMEM((HEAD_BLOCK // LANES, rows, LANES), F32),
                                   pltpu.VMEM((rows, LANES), F32)]
    o, lse = pl.pallas_call(
        kern,
        grid=(batch, n_hb, seq_len // q_block),
        in_specs=[pl.BlockSpec((None, dil, q_block, HEAD_BLOCK), lambda b, h, i: (b, 0, i, h)),
                  pl.BlockSpec((None, dil, seq_len, HEAD_BLOCK), lambda b, h, i: (b, 0, 0, n_hb + h)),
                  pl.BlockSpec((None, dil, seq_len, HEAD_BLOCK), lambda b, h, i: (b, 0, 0, 2 * n_hb + h))],
        out_specs=[pl.BlockSpec((None, rows, HEAD_BLOCK), lambda b, h, i: (b, i, h)),
                   pl.BlockSpec((None, None, rows, heads), lambda b, h, i: (h, b, i, 0))],
        out_shape=[jax.ShapeDtypeStruct((batch, seq, d), BF16),
                   jax.ShapeDtypeStruct((n_hb, batch, seq, heads), F32)],
        scratch_shapes=scratch,
        compiler_params=_cparams(("parallel", "parallel", "arbitrary")),
        name=f"banded_attn_g{group}",
    )(qkv, qkv, qkv)
    lse = jnp.transpose(lse, (1, 2, 0, 3)).reshape(batch * seq, N_HEADS)
    return o.reshape(batch * seq, d), lse


def _gqa_kernel(q_ref, k_ref, v_ref, o_ref):
    lane = lax.broadcasted_iota(I32, (Q_TILE, LANES), 1)
    grp = N_HEADS // KV_HEADS
    for kh in range(KV_HEADS):
        kcols = slice(kh * LANES, (kh + 1) * LANES)
        kd = k_ref[:, kcols]
        vd = v_ref[:, kcols]
        pairs = [q_ref[:, (kh * grp // 2 + i) * LANES:(kh * grp // 2 + i + 1) * LANES]
                 for i in range(grp // 2)]
        q4 = jnp.concatenate([_split_heads(qp) for qp in pairs], axis=0)
        s = lax.dot_general(q4, kd, (((1,), (1,)), ((), ())), preferred_element_type=F32)
        m = jnp.max(s, axis=-1, keepdims=True)
        e = jnp.exp(s - m)
        l = jnp.sum(e, axis=-1, keepdims=True)
        pv = jnp.dot(e.astype(BF16), vd, preferred_element_type=F32) / l
        for i in range(grp // 2):
            lo = pv[(2 * i) * Q_TILE:(2 * i + 1) * Q_TILE]
            hi = pv[(2 * i + 1) * Q_TILE:(2 * i + 2) * Q_TILE]
            c0 = (kh * grp // 2 + i) * LANES
            o_ref[:, c0:c0 + LANES] = jnp.where(lane < HEAD_DIM, lo, hi).astype(BF16)


def _gqa_attention(proj, batch, seq):
    d = D_MODEL
    kvw = KV_HEADS * LANES
    p3 = proj.reshape(batch, seq, proj.shape[1])
    o = pl.pallas_call(
        _gqa_kernel,
        grid=(batch, seq // Q_TILE),
        in_specs=[pl.BlockSpec((None, Q_TILE, d), lambda b, i: (b, i, 0)),
                  pl.BlockSpec((None, seq, kvw), lambda b, i: (b, 0, d // kvw)),
                  pl.BlockSpec((None, seq, kvw), lambda b, i: (b, 0, d // kvw + 1))],
        out_specs=pl.BlockSpec((None, Q_TILE, d), lambda b, i: (b, i, 0)),
        out_shape=jax.ShapeDtypeStruct((batch, seq, d), BF16),
        compiler_params=_cparams(("parallel", "arbitrary")),
        name="gqa_attn",
    )(p3, p3, p3)
    return o.reshape(batch * seq, d)


def _post_residual(x, y, g, mod):
    ms = jnp.mean(y * y, axis=-1, keepdims=True)
    return x + mod[2:3, :] * (y * lax.rsqrt(ms + NORM_EPS) * g)


def _outproj_kernel(*refs, n_groups):
    o_refs = refs[:n_groups]
    rest = refs[n_groups:]
    if n_groups > 1:
        lse_ref, rest = rest[0], rest[1:]
    x_ref, mod_ref, g_ref, w_ref, out_ref = rest
    if n_groups > 1:
        lse = [lse_ref[gi] for gi in range(n_groups)]
        m = functools.reduce(jnp.maximum, lse)
        ex = [jnp.exp(v - m) for v in lse]
        den = functools.reduce(lambda a, b: a + b, ex)
        hr = lax.broadcasted_iota(I32, (N_HEADS, D_MODEL), 0)
        hc = lax.broadcasted_iota(I32, (N_HEADS, D_MODEL), 1) // HEAD_DIM
        expand = (hr == hc).astype(F32)
        o = None
        for gi in range(n_groups):
            w = jnp.dot(ex[gi] / den, expand, preferred_element_type=F32,
                        precision=lax.Precision.HIGHEST)
            term = w * o_refs[gi][...].astype(F32)
            o = term if o is None else o + term
        o = o.astype(BF16)
    else:
        o = o_refs[0][...]
    y = jnp.dot(o, w_ref[...], preferred_element_type=F32)
    out_ref[...] = _post_residual(x_ref[...], y, g_ref[...], mod_ref[...])


def _outproj(os_, lse, x2, mod, g, w_bf, seq):
    t, d = x2.shape
    tm = 512
    per_b = seq // tm
    n_groups = len(os_)
    row_spec = pl.BlockSpec((tm, d), lambda i: (i, 0))
    in_specs = [row_spec] * n_groups
    args = list(os_)
    if n_groups > 1:
        in_specs.append(pl.BlockSpec((n_groups, tm, N_HEADS), lambda i: (0, i, 0)))
        args.append(lse)
    in_specs += [row_spec,
                 pl.BlockSpec((None, 3, d), lambda i: (i // per_b, 0, 0)),
                 pl.BlockSpec((1, d), lambda i: (0, 0)),
                 pl.BlockSpec((d, d), lambda i: (0, 0))]
    args += [x2, mod, g, w_bf]
    return pl.pallas_call(
        functools.partial(_outproj_kernel, n_groups=n_groups),
        grid=(t // tm,),
        in_specs=in_specs,
        out_specs=row_spec,
        out_shape=jax.ShapeDtypeStruct((t, d), F32),
        compiler_params=_cparams(("parallel",)),
        name=f"outproj_{n_groups}",
    )(*args)


def _router_kernel(x_ref, mod_ref, g_ref, wr_ref, br_ref, gate_ref, idx_ref):
    h = _norm_modulate(x_ref[...], g_ref[...], mod_ref[...])
    logits = jnp.dot(h, wr_ref[...], preferred_element_type=F32,
                     precision=lax.Precision.HIGHEST) + br_ref[...]
    tm = logits.shape[0]
    lane = lax.broadcasted_iota(I32, (tm, N_EXPERTS), 1)
    kcol = lax.broadcasted_iota(I32, (tm, TOP_K), 1)
    vals = jnp.zeros((tm, TOP_K), F32)
    idxs = jnp.zeros((tm, TOP_K), I32)
    cur = logits
    for k in range(TOP_K):
        m = jnp.max(cur, axis=-1, keepdims=True)
        i = jnp.min(jnp.where(cur == m, lane, N_EXPERTS), axis=-1, keepdims=True)
        vals = jnp.where(kcol == k, m, vals)
        idxs = jnp.where(kcol == k, i, idxs)
        cur = jnp.where(lane == i, -jnp.inf, cur)
    e = jnp.exp(vals - vals[:, 0:1])
    gate_ref[...] = e / jnp.sum(e, axis=-1, keepdims=True)
    idx_ref[...] = idxs


def _router(x2, mod, g, w_router, b_router, seq):
    t, d = x2.shape
    tm = 512
    per_b = seq // tm
    return pl.pallas_call(
        _router_kernel,
        grid=(t // tm,),
        in_specs=[pl.BlockSpec((tm, d), lambda i: (i, 0)),
                  pl.BlockSpec((None, 3, d), lambda i: (i // per_b, 0, 0)),
                  pl.BlockSpec((1, d), lambda i: (0, 0)),
                  pl.BlockSpec((d, N_EXPERTS), lambda i: (0, 0)),
                  pl.BlockSpec((1, N_EXPERTS), lambda i: (0, 0))],
        out_specs=[pl.BlockSpec((tm, TOP_K), lambda i: (i, 0)),
                   pl.BlockSpec((tm, TOP_K), lambda i: (i, 0))],
        out_shape=[jax.ShapeDtypeStruct((t, TOP_K), F32),
                   jax.ShapeDtypeStruct((t, TOP_K), I32)],
        compiler_params=_cparams(("parallel",)),
        name="router",
    )(x2, mod, g, w_router, b_router)


def _rank_kernel(idx_ref, pos_ref, starts_ref, cnt_ref, carry_ref):
    ph = pl.program_id(0)
    i = pl.program_id(1)
    tm = idx_ref.shape[0]
    lane = lax.broadcasted_iota(I32, (tm, N_EXPERTS), 1)
    idx = idx_ref[...]
    onehot = [(lane == idx[:, k:k+1]).astype(F32) for k in range(TOP_K)]
    cnt = functools.reduce(lambda a, b: a + b, onehot)

    @pl.when((ph == 0) & (i == 0))
    def _():
        carry_ref[...] = jnp.zeros_like(carry_ref)

    @pl.when(ph == 0)
    def _():
        carry_ref[...] += jnp.sum(cnt, axis=0, keepdims=True)

    @pl.when((ph == 1) & (i == 0))
    def _():
        total = carry_ref[...]
        cnt_ref[...] = total
        padded = jnp.ceil(total * (1.0 / EXPERT_TILE)) * EXPERT_TILE
        r = lax.broadcasted_iota(I32, (N_EXPERTS, N_EXPERTS), 0)
        c = lax.broadcasted_iota(I32, (N_EXPERTS, N_EXPERTS), 1)
        starts = jnp.dot(jnp.broadcast_to(padded, (8, N_EXPERTS)), jnp.where(r < c, 1.0, 0.0),
                         preferred_element_type=F32,
                         precision=lax.Precision.HIGHEST)[0:1]
        starts_ref[...] = starts
        carry_ref[...] = starts

    @pl.when(ph == 1)
    def _():
        r = lax.broadcasted_iota(I32, (tm, tm), 0)
        c = lax.broadcasted_iota(I32, (tm, tm), 1)
        before = jnp.dot(jnp.where(c < r, 1.0, 0.0).astype(BF16), cnt.astype(BF16),
                         preferred_element_type=F32)
        base = carry_ref[...] + before
        kcol = lax.broadcasted_iota(I32, (tm, TOP_K), 1)
        pos = jnp.zeros((tm, TOP_K), F32)
        for k in range(TOP_K):
            pos = jnp.where(kcol == k, jnp.sum(onehot[k] * base, axis=-1, keepdims=True), pos)
        pos_ref[...] = pos.astype(I32)
        carry_ref[...] += jnp.sum(cnt, axis=0, keepdims=True)


def _rank(idx):
    t = idx.shape[0]
    tm = 512
    vec = pl.BlockSpec((1, N_EXPERTS), lambda ph, i: (0, 0))
    return pl.pallas_call(
        _rank_kernel,
        grid=(2, t // tm),
        in_specs=[pl.BlockSpec((tm, TOP_K), lambda ph, i: (i, 0))],
        out_specs=[pl.BlockSpec((tm, TOP_K), lambda ph, i: (i * ph, 0)), vec, vec],
        out_shape=[jax.ShapeDtypeStruct((t, TOP_K), I32),
                   jax.ShapeDtypeStruct((1, N_EXPERTS), F32),
                   jax.ShapeDtypeStruct((1, N_EXPERTS), F32)],
        scratch_shapes=[pltpu.VMEM((1, N_EXPERTS), F32)],
        compiler_params=_cparams(("arbitrary", "arbitrary")),
        name="rank",
    )(idx)


def _row_copy(src, s, dst, d, sem):
    return pltpu.make_async_copy(src.at[pl.ds(s, 1)], dst.at[pl.ds(d, 1)], sem)


def _wait_rows(hbm, buf, sem, times, to_hbm=False):
    rows = hbm.at[pl.ds(0, buf.shape[0])]
    for _ in range(times):
        (pltpu.make_async_copy(buf, rows, sem) if to_hbm else pltpu.make_async_copy(rows, buf, sem)).wait()


def _dispatch_kernel(pos_ref, x_ref, mod_ref, g_ref, xs_in_ref, xs_ref, hbuf, sems):
    del xs_in_ref
    i = pl.program_id(0)
    last = pl.num_programs(0) - 1
    slot = i % 2
    tm = x_ref.shape[0]

    @pl.when(i >= 2)
    def _():
        _wait_rows(xs_ref, hbuf.at[slot], sems.at[slot], TOP_K, to_hbm=True)

    hbuf[slot] = _norm_modulate(x_ref[...], g_ref[...], mod_ref[...])
    base = i * (tm * TOP_K)

    def issue(t, carry):
        for k in range(TOP_K):
            _row_copy(hbuf.at[slot], t, xs_ref, pos_ref[base + t * TOP_K + k], sems.at[slot]).start()
        return carry

    lax.fori_loop(0, tm, issue, 0)

    @pl.when(i == last)
    def _():
        _wait_rows(xs_ref, hbuf.at[slot], sems.at[slot], TOP_K, to_hbm=True)
        _wait_rows(xs_ref, hbuf.at[1 - slot], sems.at[1 - slot], TOP_K, to_hbm=True)


def _dispatch(pos_flat, x2, mod, g, n_rows, seq):
    t, d = x2.shape
    tm = TOKEN_TILE
    per_b = seq // tm
    xs0 = jnp.zeros((n_rows, d), F32)
    grid_spec = pltpu.PrefetchScalarGridSpec(
        num_scalar_prefetch=1,
        grid=(t // tm,),
        in_specs=[pl.BlockSpec((tm, d), lambda i, pos: (i, 0)),
                  pl.BlockSpec((None, 3, d), lambda i, pos: (i // per_b, 0, 0)),
                  pl.BlockSpec((1, d), lambda i, pos: (0, 0)),
                  pl.BlockSpec(memory_space=pl.ANY)],
        out_specs=pl.BlockSpec(memory_space=pl.ANY),
        scratch_shapes=[pltpu.VMEM((2, tm, d), F32), pltpu.SemaphoreType.DMA((2,))],
    )
    return pl.pallas_call(
        _dispatch_kernel,
        grid_spec=grid_spec,
        out_shape=jax.ShapeDtypeStruct((n_rows, d), F32),
        input_output_aliases={4: 0},
        compiler_params=_cparams(("arbitrary",)),
        name="dispatch",
    )(pos_flat, x2, mod, g, xs0)


def _expert_kernel(te_ref, nv_ref, xs_ref, wu_ref, bu_ref, wd_ref, bd_ref, ys_ref, wu_bf, wd_bf):
    i = pl.program_id(0)
    prev = te_ref[jnp.maximum(i - 1, 0)]
    valid = i < nv_ref[0]

    @pl.when(valid & ((i == 0) | (te_ref[i] != prev)))
    def _():
        wu_bf[...] = wu_ref[...].astype(BF16)
        wd_bf[...] = wd_ref[...].astype(BF16)

    @pl.when(valid)
    def _():
        hu = jnp.dot(xs_ref[...].astype(BF16), wu_bf[...], preferred_element_type=F32) + bu_ref[...]
        x_glu = jnp.minimum(hu[:, :D_FF], SWIGLU_LIMIT)
        x_lin = jnp.clip(hu[:, D_FF:], -SWIGLU_LIMIT, SWIGLU_LIMIT)
        act = x_glu / (1.0 + jnp.exp(-SWIGLU_ALPHA * x_glu)) * (x_lin + 1.0)
        ys_ref[...] = jnp.dot(act.astype(BF16), wd_bf[...], preferred_element_type=F32) + bd_ref[...]

    @pl.when(jnp.logical_not(valid))
    def _():
        ys_ref[...] = jnp.zeros_like(ys_ref)


def _experts(tile_expert, n_valid, xs, w_up, b_up, w_down, b_down):
    n_rows, d = xs.shape
    n_tiles = n_rows // EXPERT_TILE
    grid_spec = pltpu.PrefetchScalarGridSpec(
        num_scalar_prefetch=2,
        grid=(n_tiles,),
        in_specs=[pl.BlockSpec((EXPERT_TILE, d), lambda i, te, nv: (i, 0)),
                  pl.BlockSpec((None, d, 2 * D_FF), lambda i, te, nv: (te[i], 0, 0)),
                  pl.BlockSpec((None, 1, 2 * D_FF), lambda i, te, nv: (te[i], 0, 0)),
                  pl.BlockSpec((None, D_FF, d), lambda i, te, nv: (te[i], 0, 0)),
                  pl.BlockSpec((None, 1, d), lambda i, te, nv: (te[i], 0, 0))],
        out_specs=pl.BlockSpec((EXPERT_TILE, d), lambda i, te, nv: (i, 0)),
        scratch_shapes=[pltpu.VMEM((d, 2 * D_FF), BF16), pltpu.VMEM((D_FF, d), BF16)],
    )
    return pl.pallas_call(
        _expert_kernel,
        grid_spec=grid_spec,
        out_shape=jax.ShapeDtypeStruct((n_rows, d), F32),
        compiler_params=_cparams(("arbitrary",)),
        name="experts",
    )(tile_expert, n_valid, xs, w_up, b_up.reshape(N_EXPERTS, 1, -1), w_down,
      b_down.reshape(N_EXPERTS, 1, -1))


def _combine_kernel(pos_ref, ys_ref, gate_ref, x_ref, mod_ref, g_ref, out_ref, buf, sems):
    i = pl.program_id(0)
    n_steps = pl.num_programs(0)
    tm = x_ref.shape[0]

    def fetch(step, slot):
        base = step * (tm * TOP_K)

        def issue(t, carry):
            for k in range(TOP_K):
                _row_copy(ys_ref, pos_ref[base + t * TOP_K + k], buf.at[slot, k], t, sems.at[slot]).start()
            return carry

        lax.fori_loop(0, tm, issue, 0)

    @pl.when(i == 0)
    def _():
        fetch(0, 0)

    @pl.when(i + 1 < n_steps)
    def _():
        fetch(i + 1, (i + 1) % 2)

    slot = i % 2
    for k in range(TOP_K):
        _wait_rows(ys_ref, buf.at[slot, k], sems.at[slot], 1)
    gates = gate_ref[...]
    y = gates[:, 0:1] * buf[slot, 0]
    for k in range(1, TOP_K):
        y = y + gates[:, k:k+1] * buf[slot, k]
    out_ref[...] = _post_residual(x_ref[...], y, g_ref[...], mod_ref[...])


def _combine(pos_flat, ys, gates, x2, mod, g, seq):
    t, d = x2.shape
    tm = TOKEN_TILE
    per_b = seq // tm
    grid_spec = pltpu.PrefetchScalarGridSpec(
        num_scalar_prefetch=1,
        grid=(t // tm,),
        in_specs=[pl.BlockSpec(memory_space=pl.ANY),
                  pl.BlockSpec((tm, TOP_K), lambda i, pos: (i, 0)),
                  pl.BlockSpec((tm, d), lambda i, pos: (i, 0)),
                  pl.BlockSpec((None, 3, d), lambda i, pos: (i // per_b, 0, 0)),
                  pl.BlockSpec((1, d), lambda i, pos: (0, 0))],
        out_specs=pl.BlockSpec((tm, d), lambda i, pos: (i, 0)),
        scratch_shapes=[pltpu.VMEM((2, TOP_K, tm, d), F32), pltpu.SemaphoreType.DMA((2,))],
    )
    return pl.pallas_call(
        _combine_kernel,
        grid_spec=grid_spec,
        out_shape=jax.ShapeDtypeStruct((t, d), F32),
        compiler_params=_cparams(("arbitrary",)),
        name="combine",
    )(pos_flat, ys, gates, x2, mod, g)


def _moe(x2, mod, g_pre, g_post, w_router, b_router, w_up, b_up, w_down, b_down, seq):
    t = x2.shape[0]
    gates, idx = _router(x2, mod, g_pre, w_router, b_router.reshape(1, -1), seq)
    pos, _, counts = _rank(idx)
    n_pairs = t * TOP_K
    n_tiles = n_pairs // EXPERT_TILE + N_EXPERTS
    n_rows = n_tiles * EXPERT_TILE
    tiles_per_e = (counts[0].astype(I32) + EXPERT_TILE - 1) // EXPERT_TILE
    tile_end = jnp.cumsum(tiles_per_e)
    n_valid = tile_end[-1:]
    tile_expert = jnp.sum(jnp.arange(n_tiles, dtype=I32)[:, None] >= tile_end[None, :], axis=1).astype(I32)
    last = jnp.sum(n_valid[0] - 1 >= tile_end).astype(I32)
    tile_expert = jnp.minimum(tile_expert, last)
    pos_flat = pos.reshape(n_pairs)
    xs = _dispatch(pos_flat, x2, mod, g_pre, n_rows, seq)
    ys = _experts(tile_expert, n_valid.astype(I32), xs, w_up, b_up, w_down, b_down)
    return _combine(pos_flat, ys, gates, x2, mod, g_post, seq)


def _dup_heads(w, n_heads):
    dm = w.shape[0]
    w3 = w.reshape(dm, n_heads, 1, HEAD_DIM)
    return jnp.broadcast_to(w3, (dm, n_heads, 2, HEAD_DIM)).reshape(dm, n_heads * 2 * HEAD_DIM)


def kernel(x, c, w_mod, b_mod, g_pre_mix, g_post_mix, g_pre_ffn, g_post_ffn, w_in_a, w_out_a,
           w_in_b, g_q_b, g_k_b, w_out_b, w_router, b_router, w_up, b_up, w_down, b_down):
    batch, seq, d = x.shape
    depth = w_mod.shape[0]
    x2 = x.reshape(batch * seq, d)
    mods = _modulation(c, w_mod, b_mod)
    tab1 = _rope_tables_1d(seq)
    tab2 = _rope_tables_2d(seq)
    for i in range(depth):
        j = i // 2
        mod_mix, mod_ffn = mods[2 * i], mods[2 * i + 1]
        g_pre = g_pre_mix[i].reshape(1, d)
        g_post = g_post_mix[i].reshape(1, d)
        if i % 2 == 0:
            w_bf = w_in_a[j].astype(BF16)
            outs, lses = [], []
            for g, (window, dil) in enumerate(A_GROUPS):
                qkv = _inproj_a(x2, mod_mix, g_pre, w_bf, tab1, batch, seq, g, dil)
                o, lse = _banded_attention(qkv, window, dil, batch, seq, g)
                outs.append(o)
                lses.append(lse)
            x2 = _outproj(outs, jnp.stack(lses, 0), x2, mod_mix, g_post, w_out_a[j].astype(BF16), seq)
        else:
            qd, kd = N_HEADS * HEAD_DIM, KV_HEADS * HEAD_DIM
            wb = w_in_b[j]
            w_cat = jnp.concatenate([wb[:, :qd], _dup_heads(wb[:, qd:qd + kd], KV_HEADS),
                                     _dup_heads(wb[:, qd + kd:], KV_HEADS)], axis=1).astype(BF16)
            gq = jnp.tile(g_q_b[j], LANES // HEAD_DIM).reshape(1, LANES)
            gk = jnp.tile(g_k_b[j], LANES // HEAD_DIM).reshape(1, LANES)
            proj = _inproj_b(x2, mod_mix, g_pre, w_cat, gq, gk, tab2, seq)
            o = _gqa_attention(proj, batch, seq)
            x2 = _outproj([o], None, x2, mod_mix, g_post, w_out_b[j].astype(BF16), seq)
        x2 = _moe(x2, mod_ffn, g_pre_ffn[i].reshape(1, d), g_post_ffn[i].reshape(1, d),
                  w_router[i], b_router[i], w_up[i], b_up[i], w_down[i], b_down[i], seq)
    return x2.reshape(batch, seq, d)
```

```python
import functools

import jax
import jax.numpy as jnp
import numpy as np
from jax import lax
from jax.experimental import pallas as pl
from jax.experimental.pallas import tpu as pltpu

F32 = jnp.float32
BF16 = jnp.bfloat16
I32 = jnp.int32

D_MODEL = 1024
HEAD_DIM = 64
N_HEADS = 16
KV_HEADS = 4
GRID_W = 64
A_GROUPS = ((128, 1), (512, 4), (2048, 16))
ROPE_THETA = 10000.0
N_EXPERTS = 32
TOP_K = 4
D_FF = 1024
SWIGLU_LIMIT = 7.0
SWIGLU_ALPHA = 1.702
NORM_EPS = 1e-6
LANES = 128
MXU_COLS = 256
NEG_BIG = -1e30

ROW_TILE = 1024
EXPERT_TILE = 256
TOKEN_TILE = 256
Q_TILE = 128
HEAD_BLOCK = 256
VMEM_LIMIT = 56 * 1024 * 1024


def _cparams(sem):
    return pltpu.CompilerParams(dimension_semantics=sem, vmem_limit_bytes=VMEM_LIMIT)


def _mod_kernel(c_ref, w_ref, b_ref, o_ref):
    c = c_ref[...]
    sc = c / (1.0 + jnp.exp(-c))
    o_ref[...] = jnp.dot(sc, w_ref[...], preferred_element_type=F32,
                         precision=lax.Precision.HIGHEST) + b_ref[...]


def _modulation(c, w_mod, b_mod):
    nb, d = c.shape
    n = w_mod.shape[0] * w_mod.shape[1]
    w = w_mod.reshape(n, d, 3 * d)
    b = b_mod.reshape(n, 1, 3 * d)
    out = pl.pallas_call(
        _mod_kernel,
        grid=(n, 3),
        in_specs=[pl.BlockSpec((nb, d), lambda i, j: (0, 0)),
                  pl.BlockSpec((None, d, d), lambda i, j: (i, 0, j)),
                  pl.BlockSpec((None, 1, d), lambda i, j: (i, 0, j))],
        out_specs=pl.BlockSpec((None, nb, d), lambda i, j: (i, 0, j)),
        out_shape=jax.ShapeDtypeStruct((n, nb, 3 * d), F32),
        compiler_params=_cparams(("parallel", "parallel")),
        name="modulation",
    )(c, w, b)
    return out.reshape(n, nb, 3, d)


def _norm_modulate(x, g, mod):
    ms = jnp.mean(x * x, axis=-1, keepdims=True)
    y = x * lax.rsqrt(ms + NORM_EPS) * g
    return y * (1.0 + mod[1:2, :]) + mod[0:1, :]


def _permute_pairs_1d(w):
    lead, n = w.shape[:-1], w.shape[-1] // LANES
    return w.reshape(*lead, n, 2, 2, 32).swapaxes(-3, -2).reshape(*lead, n * LANES)


def _permute_pairs_2d(w):
    lead, n = w.shape[:-1], w.shape[-1] // LANES
    return jnp.moveaxis(w.reshape(*lead, n, 2, 2, 2, 16), -2, -4).reshape(*lead, n * LANES)


def _rotary(a, cos, sin_signed):
    return a * cos + pltpu.roll(a, LANES // 2, 1) * sin_signed


def _rope_tables_1d(seq):
    lane = np.arange(LANES)
    inv = 1.0 / (ROPE_THETA ** (jnp.arange(0, HEAD_DIM, 2, dtype=F32) / HEAD_DIM))
    ang = jnp.arange(seq, dtype=F32)[:, None] * inv[None, :]
    cos = jnp.cos(ang)[:, lane % 32]
    sin = jnp.sin(ang)[:, lane % 32]
    return cos, jnp.where(jnp.asarray(lane < 64)[None, :], -sin, sin)


def _rope_tables_2d(seq):
    half = HEAD_DIM // 2
    lane = np.arange(LANES)
    inv = 1.0 / (ROPE_THETA ** (jnp.arange(0, half, 2, dtype=F32) / half))
    pos = jnp.arange(seq)
    row_ang = (pos // GRID_W).astype(F32)[:, None] * inv[None, :]
    col_ang = (pos % GRID_W).astype(F32)[:, None] * inv[None, :]
    f = lane % 16
    is_row = jnp.asarray((lane % 32) < 16)[None, :]
    cos = jnp.where(is_row, jnp.cos(row_ang)[:, f], jnp.cos(col_ang)[:, f])
    sin = jnp.where(is_row, jnp.sin(row_ang)[:, f], jnp.sin(col_ang)[:, f])
    return cos, jnp.where(jnp.asarray(lane < 64)[None, :], -sin, sin)


def _residue_major(table, tile, dil):
    seq, w = table.shape
    n = tile // dil
    return table.reshape(seq // tile, n, dil, w).swapaxes(1, 2).reshape(seq, w)


def _inproj_a_kernel(x_ref, mod_ref, g_ref, w_ref, cos_ref, sin_ref, o_ref, h_ref, *scratch, dil):
    j = pl.program_id(1)
    tm = x_ref.shape[0]
    n = tm // dil

    @pl.when(j == 0)
    def _():
        h = _norm_modulate(x_ref[...], g_ref[...], mod_ref[...])
        if dil == 1:
            h_ref[...] = h.astype(BF16)
        else:
            (stage,) = scratch
            for c in range(D_MODEL // LANES):
                stage[c] = h[:, c * LANES:(c + 1) * LANES]
            for c in range(D_MODEL // LANES):
                for r in range(dil):
                    h_ref[r * n:(r + 1) * n, c * LANES:(c + 1) * LANES] = (
                        stage[c, pl.ds(r, n, stride=dil), :].astype(BF16))

    def store(sl, val):
        for r in range(dil):
            o_ref[r, :, sl] = val[r * n:(r + 1) * n]

    def matmul_chunks():
        for c in range(D_MODEL // MXU_COLS):
            cols = slice(c * MXU_COLS, (c + 1) * MXU_COLS)
            yield c, jnp.dot(h_ref[...], w_ref[:, cols], preferred_element_type=F32)

    @pl.when(j == 2)
    def _():
        for c, acc in matmul_chunks():
            store(slice(c * MXU_COLS, (c + 1) * MXU_COLS), acc.astype(BF16))

    @pl.when(j != 2)
    def _():
        scale = jnp.where(j == 0, HEAD_DIM ** -0.5, 1.0).astype(F32)
        cos, sin = cos_ref[...] * scale, sin_ref[...] * scale
        for c, acc in matmul_chunks():
            for s in range(MXU_COLS // LANES):
                sub = acc[:, s * LANES:(s + 1) * LANES]
                lo = c * MXU_COLS + s * LANES
                store(slice(lo, lo + LANES), _rotary(sub, cos, sin).astype(BF16))


def _inproj_a(x2, mod, g, w_bf, tables, batch, seq, group, dil):
    t, d = x2.shape
    tm = ROW_TILE
    per_b = seq // tm
    n = tm // dil
    tabs = [_residue_major(tb, tm, dil) for tb in tables]
    tab_spec = pl.BlockSpec((tm, LANES), lambda i, j: (i % per_b, 0))
    return pl.pallas_call(
        functools.partial(_inproj_a_kernel, dil=dil),
        grid=(t // tm, 3),
        in_specs=[pl.BlockSpec((tm, d), lambda i, j: (i, 0)),
                  pl.BlockSpec((None, 3, d), lambda i, j: (i // per_b, 0, 0)),
                  pl.BlockSpec((1, d), lambda i, j: (0, 0)),
                  pl.BlockSpec((d, d), lambda i, j: (0, 3 * group + j)),
                  tab_spec, tab_spec],
        out_specs=pl.BlockSpec((None, dil, n, d), lambda i, j: (i // per_b, 0, i % per_b, j)),
        out_shape=jax.ShapeDtypeStruct((batch, dil, seq // dil, 3 * d), BF16),
        scratch_shapes=[pltpu.VMEM((tm, d), BF16)]
        + ([] if dil == 1 else [pltpu.VMEM((d // LANES, tm, LANES), F32)]),
        compiler_params=_cparams(("parallel", "arbitrary")),
        name=f"inproj_a_g{group}",
    )(x2, mod, g, w_bf, *tabs)


def _lane_head(shape, dim):
    return (lax.broadcasted_iota(I32, shape, dim) % HEAD_DIM) // (HEAD_DIM // 2)


def _head_rms(a, g):
    same_head = _lane_head((LANES, LANES), 0) == _lane_head((LANES, LANES), 1)
    ss = jnp.dot(a * a, same_head.astype(F32), preferred_element_type=F32,
                 precision=lax.Precision.HIGHEST)
    return a * lax.rsqrt(ss * (1.0 / HEAD_DIM) + NORM_EPS) * g


def _inproj_b_kernel(x_ref, mod_ref, g_ref, w_ref, gq_ref, gk_ref, cos_ref, sin_ref, o_ref, h_ref):
    j = pl.program_id(1)

    @pl.when(j == 0)
    def _():
        h_ref[...] = _norm_modulate(x_ref[...], g_ref[...], mod_ref[...]).astype(BF16)

    cos, sin = cos_ref[...], sin_ref[...]

    def chunks(plain_from):
        for c in range(D_MODEL // MXU_COLS):
            acc = jnp.dot(h_ref[...], w_ref[:, c * MXU_COLS:(c + 1) * MXU_COLS],
                          preferred_element_type=F32)
            for s in range(MXU_COLS // LANES):
                lo = c * MXU_COLS + s * LANES
                yield slice(lo, lo + LANES), acc[:, s * LANES:(s + 1) * LANES], c >= plain_from

    @pl.when(j == 0)
    def _():
        for sl, sub, _ in chunks(D_MODEL // MXU_COLS):
            y = _rotary(_head_rms(sub, gq_ref[...]), cos, sin)
            o_ref[:, sl] = (y * HEAD_DIM ** -0.5).astype(BF16)

    @pl.when(j == 1)
    def _():
        for sl, sub, plain in chunks(D_MODEL // MXU_COLS // 2):
            if plain:
                o_ref[:, sl] = sub.astype(BF16)
            else:
                o_ref[:, sl] = _rotary(_head_rms(sub, gk_ref[...]), cos, sin).astype(BF16)


def _inproj_b(x2, mod, g, w_bf, gq, gk, tables, seq):
    t, d = x2.shape
    n = w_bf.shape[1]
    tm = ROW_TILE
    per_b = seq // tm
    tab_spec = pl.BlockSpec((tm, LANES), lambda i, j: (i % per_b, 0))
    vec_spec = pl.BlockSpec((1, LANES), lambda i, j: (0, 0))
    return pl.pallas_call(
        _inproj_b_kernel,
        grid=(t // tm, n // d),
        in_specs=[pl.BlockSpec((tm, d), lambda i, j: (i, 0)),
                  pl.BlockSpec((None, 3, d), lambda i, j: (i // per_b, 0, 0)),
                  pl.BlockSpec((1, d), lambda i, j: (0, 0)),
                  pl.BlockSpec((d, d), lambda i, j: (0, j)),
                  vec_spec, vec_spec, tab_spec, tab_spec],
        out_specs=pl.BlockSpec((tm, d), lambda i, j: (i, j)),
        out_shape=jax.ShapeDtypeStruct((t, n), BF16),
        scratch_shapes=[pltpu.VMEM((tm, d), BF16)],
        compiler_params=_cparams(("parallel", "arbitrary")),
        name="inproj_b",
    )(x2, mod, g, w_bf, gq, gk, *tables)


def _split_heads(qp):
    head = _lane_head(qp.shape, 1)
    zero = jnp.zeros_like(qp)
    return [jnp.where(head == 0, qp, zero), jnp.where(head == 1, qp, zero)]


def _banded_kernel(q_ref, k_ref, v_ref, o_ref, lse_ref, *scratch, dil, seq_len, q_block, half):
    qi = pl.program_id(2)
    kw = min(2 * Q_TILE, seq_len)
    heads = HEAD_BLOCK // HEAD_DIM
    lane = lax.broadcasted_iota(I32, (Q_TILE, LANES), 1)
    hcol = lax.broadcasted_iota(I32, (Q_TILE, LANES), 1)
    if dil > 1:
        o_acc, lse_acc = scratch
    for r in range(dil):
        for t in range(q_block // Q_TILE):
            q0 = qi * q_block + t * Q_TILE
            kstart = pl.multiple_of(jnp.clip(q0 - half, 0, seq_len - kw), HEAD_DIM)
            qpos = q0 + lax.broadcasted_iota(I32, (Q_TILE, kw), 0)
            kpos = kstart + lax.broadcasted_iota(I32, (Q_TILE, kw), 1)
            band = jnp.abs(qpos - kpos) <= half
            band2 = jnp.concatenate([band, band], axis=0)
            if dil == 1:
                rows = slice(t * Q_TILE, (t + 1) * Q_TILE)
            else:
                rows = pl.ds(t * Q_TILE * dil + r, Q_TILE, stride=dil)
            lse_t = jnp.zeros((Q_TILE, LANES), F32)
            for p in range(HEAD_BLOCK // LANES):
                cols = slice(p * LANES, (p + 1) * LANES)
                q2 = jnp.concatenate(_split_heads(q_ref[r, t * Q_TILE:(t + 1) * Q_TILE, cols]),
                                     axis=0)
                kp = k_ref[r, pl.ds(kstart, kw), cols]
                vp = v_ref[r, pl.ds(kstart, kw), cols]
                s = lax.dot_general(q2, kp, (((1,), (1,)), ((), ())), preferred_element_type=F32)
                s = jnp.where(band2, s, NEG_BIG)
                m = jnp.max(s, axis=-1, keepdims=True)
                e = jnp.exp(s - m)
                l = jnp.sum(e, axis=-1, keepdims=True)
                pv = jnp.dot(e.astype(BF16), vp, preferred_element_type=F32) / l
                o = jnp.where(lane < HEAD_DIM, pv[:Q_TILE], pv[Q_TILE:])
                if dil == 1:
                    o_ref[rows, cols] = o.astype(BF16)
                else:
                    o_acc[p, rows, :] = o
                lse = m + jnp.log(l)
                lse_t = jnp.where(hcol == 2 * p, lse[:Q_TILE], lse_t)
                lse_t = jnp.where(hcol == 2 * p + 1, lse[Q_TILE:], lse_t)
            if dil == 1:
                lse_ref[rows, :] = lse_t[:, :heads]
            else:
                lse_acc[rows, :] = lse_t
    if dil > 1:
        for p in range(HEAD_BLOCK // LANES):
            o_ref[:, p * LANES:(p + 1) * LANES] = o_acc[p].astype(BF16)
        lse_ref[...] = lse_acc[:, :heads]


def _banded_attention(qkv, window, dil, batch, seq, group):
    d = D_MODEL
    seq_len = seq // dil
    half = window // (2 * dil)
    q_block = max(Q_TILE, min(seq_len, 512 // dil))
    rows = q_block * dil
    n_hb = d // HEAD_BLOCK
    heads = HEAD_BLOCK // HEAD_DIM
    kern = functools.partial(_banded_kernel, dil=dil, seq_len=seq_len, q_block=q_block, half=half)
    scratch = [] if dil == 1 else [pltpu.VMEM((HEAD_BLOCK // LANES, rows, LANES), F32),
                                   pltpu.VMEM((rows, LANES), F32)]
    o, lse = pl.pallas_call(
        kern,
        grid=(batch, n_hb, seq_len // q_block),
        in_specs=[pl.BlockSpec((None, dil, q_block, HEAD_BLOCK), lambda b, h, i: (b, 0, i, h)),
                  pl.BlockSpec((None, dil, seq_len, HEAD_BLOCK), lambda b, h, i: (b, 0, 0, n_hb + h)),
                  pl.BlockSpec((None, dil, seq_len, HEAD_BLOCK), lambda b, h, i: (b, 0, 0, 2 * n_hb + h))],
        out_specs=[pl.BlockSpec((None, rows, HEAD_BLOCK), lambda b, h, i: (b, i, h)),
                   pl.BlockSpec((None, None, rows, heads), lambda b, h, i: (h, b, i, 0))],
        out_shape=[jax.ShapeDtypeStruct((batch, seq, d), BF16),
                   jax.ShapeDtypeStruct((n_hb, batch, seq, heads), F32)],
        scratch_shapes=scratch,
        compiler_params=_cparams(("parallel", "parallel", "arbitrary")),
        name=f"banded_attn_g{group}",
    )(qkv, qkv, qkv)
    lse = jnp.transpose(lse, (1, 2, 0, 3)).reshape(batch * seq, N_HEADS)
    return o.reshape(batch * seq, d), lse


def _gqa_kernel(q_ref, k_ref, v_ref, o_ref):
    lane = lax.broadcasted_iota(I32, (Q_TILE, LANES), 1)
    vlane = lax.broadcasted_iota(I32, (v_ref.shape[0], LANES), 1)
    one = jnp.ones((), BF16)
    n_pairs = N_HEADS // KV_HEADS // 2
    for kh in range(KV_HEADS):
        kcols = slice(kh * LANES, (kh + 1) * LANES)
        kd = k_ref[:, kcols]
        vd = v_ref[:, kcols]
        vmat = jnp.concatenate([jnp.where(vlane < HEAD_DIM, vd, one),
                                jnp.where(vlane < HEAD_DIM, one, vd)], axis=1)
        split = [_split_heads(q_ref[:, (kh * n_pairs + i) * LANES:(kh * n_pairs + i + 1) * LANES])
                 for i in range(n_pairs)]
        qs = jnp.concatenate([sp[0] for sp in split] + [sp[1] for sp in split], axis=0)
        s = lax.dot_general(qs, kd, (((1,), (1,)), ((), ())), preferred_element_type=F32)
        m = jnp.max(s, axis=-1, keepdims=True)
        p = jnp.exp((s - m).astype(BF16))
        pv = jnp.dot(p, vmat, preferred_element_type=F32)
        half_rows = n_pairs * Q_TILE
        first, second = pv[:half_rows, :LANES], pv[half_rows:, LANES:]
        first = first / pltpu.roll(first, HEAD_DIM, 1)
        second = second / pltpu.roll(second, HEAD_DIM, 1)
        for i in range(n_pairs):
            rows = slice(i * Q_TILE, (i + 1) * Q_TILE)
            c0 = (kh * n_pairs + i) * LANES
            o_ref[:, c0:c0 + LANES] = jnp.where(lane < HEAD_DIM, first[rows], second[rows]).astype(BF16)


def _gqa_attention(proj, batch, seq):
    d = D_MODEL
    kvw = KV_HEADS * LANES
    p3 = proj.reshape(batch, seq, proj.shape[1])
    o = pl.pallas_call(
        _gqa_kernel,
        grid=(batch, seq // Q_TILE),
        in_specs=[pl.BlockSpec((None, Q_TILE, d), lambda b, i: (b, i, 0)),
                  pl.BlockSpec((None, seq, kvw), lambda b, i: (b, 0, d // kvw)),
                  pl.BlockSpec((None, seq, kvw), lambda b, i: (b, 0, d // kvw + 1))],
        out_specs=pl.BlockSpec((None, Q_TILE, d), lambda b, i: (b, i, 0)),
        out_shape=jax.ShapeDtypeStruct((batch, seq, d), BF16),
        compiler_params=_cparams(("parallel", "arbitrary")),
        name="gqa_attn",
    )(p3, p3, p3)
    return o.reshape(batch * seq, d)


def _post_residual(x, y, g, mod):
    ms = jnp.mean(y * y, axis=-1, keepdims=True)
    return x + mod[2:3, :] * (y * lax.rsqrt(ms + NORM_EPS) * g)


def _outproj_kernel(*refs, n_groups):
    o_refs = refs[:n_groups]
    rest = refs[n_groups:]
    if n_groups > 1:
        lse_ref, rest = rest[0], rest[1:]
    x_ref, mod_ref, g_ref, w_ref, out_ref = rest
    if n_groups > 1:
        lse = [lse_ref[gi] for gi in range(n_groups)]
        m = functools.reduce(jnp.maximum, lse)
        ex = [jnp.exp(v - m) for v in lse]
        den = functools.reduce(lambda a, b: a + b, ex)
        hr = lax.broadcasted_iota(I32, (2 * N_HEADS, D_MODEL), 0) % N_HEADS
        hc = lax.broadcasted_iota(I32, (2 * N_HEADS, D_MODEL), 1) // HEAD_DIM
        expand = jnp.where(hr == hc, 1.0, 0.0).astype(BF16)
        o = None
        for gi in range(n_groups):
            wg = ex[gi] / den
            w_hi = wg.astype(BF16)
            w_lo = (wg - w_hi.astype(F32)).astype(BF16)
            w = jnp.dot(jnp.concatenate([w_hi, w_lo], axis=-1), expand, preferred_element_type=F32)
            term = w * o_refs[gi][...].astype(F32)
            o = term if o is None else o + term
        o = o.astype(BF16)
    else:
        o = o_refs[0][...]
    y = jnp.dot(o, w_ref[...], preferred_element_type=F32)
    out_ref[...] = _post_residual(x_ref[...], y, g_ref[...], mod_ref[...])


def _outproj(os_, lse, x2, mod, g, w_bf, seq):
    t, d = x2.shape
    tm = 512
    per_b = seq // tm
    n_groups = len(os_)
    row_spec = pl.BlockSpec((tm, d), lambda i: (i, 0))
    in_specs = [row_spec] * n_groups
    args = list(os_)
    if n_groups > 1:
        in_specs.append(pl.BlockSpec((n_groups, tm, N_HEADS), lambda i: (0, i, 0)))
        args.append(lse)
    in_specs += [row_spec,
                 pl.BlockSpec((None, 3, d), lambda i: (i // per_b, 0, 0)),
                 pl.BlockSpec((1, d), lambda i: (0, 0)),
                 pl.BlockSpec((d, d), lambda i: (0, 0))]
    args += [x2, mod, g, w_bf]
    return pl.pallas_call(
        functools.partial(_outproj_kernel, n_groups=n_groups),
        grid=(t // tm,),
        in_specs=in_specs,
        out_specs=row_spec,
        out_shape=jax.ShapeDtypeStruct((t, d), F32),
        compiler_params=_cparams(("parallel",)),
        name=f"outproj_{n_groups}",
    )(*args)


def _router_kernel(x_ref, mod_ref, g_ref, wr_ref, br_ref, gate_ref, idx_ref):
    h = _norm_modulate(x_ref[...], g_ref[...], mod_ref[...])
    logits = jnp.dot(h, wr_ref[...], preferred_element_type=F32,
                     precision=lax.Precision.HIGHEST) + br_ref[...]
    tm = logits.shape[0]
    lane = lax.broadcasted_iota(I32, (tm, N_EXPERTS), 1)
    kcol = lax.broadcasted_iota(I32, (tm, TOP_K), 1)
    vals = jnp.zeros((tm, TOP_K), F32)
    idxs = jnp.zeros((tm, TOP_K), I32)
    cur = logits
    for k in range(TOP_K):
        m = jnp.max(cur, axis=-1, keepdims=True)
        i = jnp.min(jnp.where(cur == m, lane, N_EXPERTS), axis=-1, keepdims=True)
        vals = jnp.where(kcol == k, m, vals)
        idxs = jnp.where(kcol == k, i, idxs)
        cur = jnp.where(lane == i, -jnp.inf, cur)
    e = jnp.exp(vals - vals[:, 0:1])
    gate_ref[...] = e / jnp.sum(e, axis=-1, keepdims=True)
    idx_ref[...] = idxs


def _router(x2, mod, g, w_router, b_router, seq):
    t, d = x2.shape
    tm = 512
    per_b = seq // tm
    return pl.pallas_call(
        _router_kernel,
        grid=(t // tm,),
        in_specs=[pl.BlockSpec((tm, d), lambda i: (i, 0)),
                  pl.BlockSpec((None, 3, d), lambda i: (i // per_b, 0, 0)),
                  pl.BlockSpec((1, d), lambda i: (0, 0)),
                  pl.BlockSpec((d, N_EXPERTS), lambda i: (0, 0)),
                  pl.BlockSpec((1, N_EXPERTS), lambda i: (0, 0))],
        out_specs=[pl.BlockSpec((tm, TOP_K), lambda i: (i, 0)),
                   pl.BlockSpec((tm, TOP_K), lambda i: (i, 0))],
        out_shape=[jax.ShapeDtypeStruct((t, TOP_K), F32),
                   jax.ShapeDtypeStruct((t, TOP_K), I32)],
        compiler_params=_cparams(("parallel",)),
        name="router",
    )(x2, mod, g, w_router, b_router)


def _rank_kernel(idx_ref, pos_ref, starts_ref, cnt_ref, carry_ref):
    ph = pl.program_id(0)
    i = pl.program_id(1)
    tm = idx_ref.shape[0]
    lane = lax.broadcasted_iota(I32, (tm, N_EXPERTS), 1)
    idx = idx_ref[...]
    onehot = [(lane == idx[:, k:k+1]).astype(F32) for k in range(TOP_K)]
    cnt = functools.reduce(lambda a, b: a + b, onehot)

    @pl.when((ph == 0) & (i == 0))
    def _():
        carry_ref[...] = jnp.zeros_like(carry_ref)

    @pl.when(ph == 0)
    def _():
        carry_ref[...] += jnp.sum(cnt, axis=0, keepdims=True)

    @pl.when((ph == 1) & (i == 0))
    def _():
        total = carry_ref[...]
        cnt_ref[...] = total
        padded = jnp.ceil(total * (1.0 / EXPERT_TILE)) * EXPERT_TILE
        r = lax.broadcasted_iota(I32, (N_EXPERTS, N_EXPERTS), 0)
        c = lax.broadcasted_iota(I32, (N_EXPERTS, N_EXPERTS), 1)
        starts = jnp.dot(jnp.broadcast_to(padded, (8, N_EXPERTS)), jnp.where(r < c, 1.0, 0.0),
                         preferred_element_type=F32,
                         precision=lax.Precision.HIGHEST)[0:1]
        starts_ref[...] = starts
        carry_ref[...] = starts

    @pl.when(ph == 1)
    def _():
        r = lax.broadcasted_iota(I32, (tm, tm), 0)
        c = lax.broadcasted_iota(I32, (tm, tm), 1)
        before = jnp.dot(jnp.where(c < r, 1.0, 0.0).astype(BF16), cnt.astype(BF16),
                         preferred_element_type=F32)
        base = carry_ref[...] + before
        kcol = lax.broadcasted_iota(I32, (tm, TOP_K), 1)
        pos = jnp.zeros((tm, TOP_K), F32)
        for k in range(TOP_K):
            pos = jnp.where(kcol == k, jnp.sum(onehot[k] * base, axis=-1, keepdims=True), pos)
        pos_ref[...] = pos.astype(I32)
        carry_ref[...] += jnp.sum(cnt, axis=0, keepdims=True)


def _rank(idx):
    t = idx.shape[0]
    tm = 512
    vec = pl.BlockSpec((1, N_EXPERTS), lambda ph, i: (0, 0))
    return pl.pallas_call(
        _rank_kernel,
        grid=(2, t // tm),
        in_specs=[pl.BlockSpec((tm, TOP_K), lambda ph, i: (i, 0))],
        out_specs=[pl.BlockSpec((tm, TOP_K), lambda ph, i: (i * ph, 0)), vec, vec],
        out_shape=[jax.ShapeDtypeStruct((t, TOP_K), I32),
                   jax.ShapeDtypeStruct((1, N_EXPERTS), F32),
                   jax.ShapeDtypeStruct((1, N_EXPERTS), F32)],
        scratch_shapes=[pltpu.VMEM((1, N_EXPERTS), F32)],
        compiler_params=_cparams(("arbitrary", "arbitrary")),
        name="rank",
    )(idx)


def _row_copy(src, s, dst, d, sem):
    return pltpu.make_async_copy(src.at[pl.ds(s, 1)], dst.at[pl.ds(d, 1)], sem)


def _wait_rows(hbm, buf, sem, times, to_hbm=False):
    rows = hbm.at[pl.ds(0, buf.shape[0])]
    for _ in range(times):
        (pltpu.make_async_copy(buf, rows, sem) if to_hbm else pltpu.make_async_copy(rows, buf, sem)).wait()


def _dispatch_kernel(pos_ref, x_ref, mod_ref, g_ref, xs_in_ref, xs_ref, hbuf, sems):
    del xs_in_ref
    i = pl.program_id(0)
    last = pl.num_programs(0) - 1
    slot = i % 2
    tm = x_ref.shape[0]

    @pl.when(i >= 2)
    def _():
        _wait_rows(xs_ref, hbuf.at[slot], sems.at[slot], TOP_K, to_hbm=True)

    hbuf[slot] = _norm_modulate(x_ref[...], g_ref[...], mod_ref[...])
    base = i * (tm * TOP_K)

    def issue(t, carry):
        for k in range(TOP_K):
            _row_copy(hbuf.at[slot], t, xs_ref, pos_ref[base + t * TOP_K + k], sems.at[slot]).start()
        return carry

    lax.fori_loop(0, tm, issue, 0)

    @pl.when(i == last)
    def _():
        _wait_rows(xs_ref, hbuf.at[slot], sems.at[slot], TOP_K, to_hbm=True)
        _wait_rows(xs_ref, hbuf.at[1 - slot], sems.at[1 - slot], TOP_K, to_hbm=True)


def _dispatch(pos_flat, x2, mod, g, n_rows, seq):
    t, d = x2.shape
    tm = TOKEN_TILE
    per_b = seq // tm
    xs0 = jnp.zeros((n_rows, d), F32)
    grid_spec = pltpu.PrefetchScalarGridSpec(
        num_scalar_prefetch=1,
        grid=(t // tm,),
        in_specs=[pl.BlockSpec((tm, d), lambda i, pos: (i, 0)),
                  pl.BlockSpec((None, 3, d), lambda i, pos: (i // per_b, 0, 0)),
                  pl.BlockSpec((1, d), lambda i, pos: (0, 0)),
                  pl.BlockSpec(memory_space=pl.ANY)],
        out_specs=pl.BlockSpec(memory_space=pl.ANY),
        scratch_shapes=[pltpu.VMEM((2, tm, d), F32), pltpu.SemaphoreType.DMA((2,))],
    )
    return pl.pallas_call(
        _dispatch_kernel,
        grid_spec=grid_spec,
        out_shape=jax.ShapeDtypeStruct((n_rows, d), F32),
        input_output_aliases={4: 0},
        compiler_params=_cparams(("arbitrary",)),
        name="dispatch",
    )(pos_flat, x2, mod, g, xs0)


def _expert_kernel(te_ref, nv_ref, xs_ref, wu_ref, bu_ref, wd_ref, bd_ref, ys_ref, wu_bf, wd_bf):
    i = pl.program_id(0)
    prev = te_ref[jnp.maximum(i - 1, 0)]
    valid = i < nv_ref[0]

    @pl.when(valid & ((i == 0) | (te_ref[i] != prev)))
    def _():
        wu_bf[...] = wu_ref[...].astype(BF16)
        wd_bf[...] = wd_ref[...].astype(BF16)

    @pl.when(valid)
    def _():
        hu = jnp.dot(xs_ref[...].astype(BF16), wu_bf[...], preferred_element_type=F32) + bu_ref[...]
        x_glu = jnp.minimum(hu[:, :D_FF], SWIGLU_LIMIT)
        x_lin = jnp.clip(hu[:, D_FF:], -SWIGLU_LIMIT, SWIGLU_LIMIT)
        act = x_glu / (1.0 + jnp.exp(-SWIGLU_ALPHA * x_glu)) * (x_lin + 1.0)
        ys_ref[...] = jnp.dot(act.astype(BF16), wd_bf[...], preferred_element_type=F32) + bd_ref[...]

    @pl.when(jnp.logical_not(valid))
    def _():
        ys_ref[...] = jnp.zeros_like(ys_ref)


def _experts(tile_expert, n_valid, xs, w_up, b_up, w_down, b_down, layer):
    n_rows, d = xs.shape
    n_tiles = n_rows // EXPERT_TILE
    grid_spec = pltpu.PrefetchScalarGridSpec(
        num_scalar_prefetch=2,
        grid=(n_tiles,),
        in_specs=[pl.BlockSpec((EXPERT_TILE, d), lambda i, te, nv: (i, 0)),
                  pl.BlockSpec((None, None, d, 2 * D_FF), lambda i, te, nv: (layer, te[i], 0, 0)),
                  pl.BlockSpec((None, 1, 2 * D_FF), lambda i, te, nv: (te[i], 0, 0)),
                  pl.BlockSpec((None, None, D_FF, d), lambda i, te, nv: (layer, te[i], 0, 0)),
                  pl.BlockSpec((None, 1, d), lambda i, te, nv: (te[i], 0, 0))],
        out_specs=pl.BlockSpec((EXPERT_TILE, d), lambda i, te, nv: (i, 0)),
        scratch_shapes=[pltpu.VMEM((d, 2 * D_FF), BF16), pltpu.VMEM((D_FF, d), BF16)],
    )
    return pl.pallas_call(
        _expert_kernel,
        grid_spec=grid_spec,
        out_shape=jax.ShapeDtypeStruct((n_rows, d), F32),
        compiler_params=_cparams(("arbitrary",)),
        name="experts",
    )(tile_expert, n_valid, xs, w_up, b_up.reshape(N_EXPERTS, 1, -1), w_down,
      b_down.reshape(N_EXPERTS, 1, -1))


def _combine_kernel(pos_ref, ys_ref, gate_ref, x_ref, mod_ref, g_ref, out_ref, buf, sems):
    i = pl.program_id(0)
    n_steps = pl.num_programs(0)
    tm = x_ref.shape[0]

    def fetch(step, slot):
        base = step * (tm * TOP_K)

        def issue(t, carry):
            for k in range(TOP_K):
                _row_copy(ys_ref, pos_ref[base + t * TOP_K + k], buf.at[slot, k], t, sems.at[slot]).start()
            return carry

        lax.fori_loop(0, tm, issue, 0)

    @pl.when(i == 0)
    def _():
        fetch(0, 0)

    @pl.when(i + 1 < n_steps)
    def _():
        fetch(i + 1, (i + 1) % 2)

    slot = i % 2
    for k in range(TOP_K):
        _wait_rows(ys_ref, buf.at[slot, k], sems.at[slot], 1)
    gates = gate_ref[...]
    y = gates[:, 0:1] * buf[slot, 0]
    for k in range(1, TOP_K):
        y = y + gates[:, k:k+1] * buf[slot, k]
    out_ref[...] = _post_residual(x_ref[...], y, g_ref[...], mod_ref[...])


def _combine(pos_flat, ys, gates, x2, mod, g, seq):
    t, d = x2.shape
    tm = TOKEN_TILE
    per_b = seq // tm
    grid_spec = pltpu.PrefetchScalarGridSpec(
        num_scalar_prefetch=1,
        grid=(t // tm,),
        in_specs=[pl.BlockSpec(memory_space=pl.ANY),
                  pl.BlockSpec((tm, TOP_K), lambda i, pos: (i, 0)),
                  pl.BlockSpec((tm, d), lambda i, pos: (i, 0)),
                  pl.BlockSpec((None, 3, d), lambda i, pos: (i // per_b, 0, 0)),
                  pl.BlockSpec((1, d), lambda i, pos: (0, 0))],
        out_specs=pl.BlockSpec((tm, d), lambda i, pos: (i, 0)),
        scratch_shapes=[pltpu.VMEM((2, TOP_K, tm, d), F32), pltpu.SemaphoreType.DMA((2,))],
    )
    return pl.pallas_call(
        _combine_kernel,
        grid_spec=grid_spec,
        out_shape=jax.ShapeDtypeStruct((t, d), F32),
        compiler_params=_cparams(("arbitrary",)),
        name="combine",
    )(pos_flat, ys, gates, x2, mod, g)


def _moe(x2, mod, g_pre, g_post, w_router, b_router, w_up, b_up, w_down, b_down, seq, layer):
    t = x2.shape[0]
    gates, idx = _router(x2, mod, g_pre, w_router, b_router.reshape(1, -1), seq)
    pos, _, counts = _rank(idx)
    n_pairs = t * TOP_K
    n_tiles = n_pairs // EXPERT_TILE + N_EXPERTS
    n_rows = n_tiles * EXPERT_TILE
    tiles_per_e = (counts[0].astype(I32) + EXPERT_TILE - 1) // EXPERT_TILE
    tile_end = jnp.cumsum(tiles_per_e)
    n_valid = tile_end[-1:]
    tile_expert = jnp.sum(jnp.arange(n_tiles, dtype=I32)[:, None] >= tile_end[None, :], axis=1).astype(I32)
    last = jnp.sum(n_valid[0] - 1 >= tile_end).astype(I32)
    tile_expert = jnp.minimum(tile_expert, last)
    pos_flat = pos.reshape(n_pairs)
    xs = _dispatch(pos_flat, x2, mod, g_pre, n_rows, seq)
    ys = _experts(tile_expert, n_valid.astype(I32), xs, w_up, b_up, w_down, b_down, layer)
    return _combine(pos_flat, ys, gates, x2, mod, g_post, seq)


def _dup_heads(w, n_heads):
    dm = w.shape[0]
    w3 = w.reshape(dm, n_heads, 1, HEAD_DIM)
    return jnp.broadcast_to(w3, (dm, n_heads, 2, HEAD_DIM)).reshape(dm, n_heads * 2 * HEAD_DIM)


def kernel(x, c, w_mod, b_mod, g_pre_mix, g_post_mix, g_pre_ffn, g_post_ffn, w_in_a, w_out_a,
           w_in_b, g_q_b, g_k_b, w_out_b, w_router, b_router, w_up, b_up, w_down, b_down):
    batch, seq, d = x.shape
    depth = w_mod.shape[0]
    x2 = x.reshape(batch * seq, d)
    mods = _modulation(c, w_mod, b_mod)
    tab1 = _rope_tables_1d(seq)
    tab2 = _rope_tables_2d(seq)
    for i in range(depth):
        j = i // 2
        mod_mix, mod_ffn = mods[2 * i], mods[2 * i + 1]
        g_pre = g_pre_mix[i].reshape(1, d)
        g_post = g_post_mix[i].reshape(1, d)
        if i % 2 == 0:
            w4 = w_in_a[j].reshape(d, len(A_GROUPS), 3, d)
            w_bf = jnp.concatenate([_permute_pairs_1d(w4[:, :, :2]), w4[:, :, 2:]], axis=2)
            w_bf = w_bf.reshape(d, -1).astype(BF16)
            outs, lses = [], []
            for g, (window, dil) in enumerate(A_GROUPS):
                qkv = _inproj_a(x2, mod_mix, g_pre, w_bf, tab1, batch, seq, g, dil)
                o, lse = _banded_attention(qkv, window, dil, batch, seq, g)
                outs.append(o)
                lses.append(lse)
            x2 = _outproj(outs, jnp.stack(lses, 0), x2, mod_mix, g_post, w_out_a[j].astype(BF16), seq)
        else:
            qd, kd = N_HEADS * HEAD_DIM, KV_HEADS * HEAD_DIM
            wb = w_in_b[j]
            w_cat = jnp.concatenate([_permute_pairs_2d(wb[:, :qd]),
                                     _permute_pairs_2d(_dup_heads(wb[:, qd:qd + kd], KV_HEADS)),
                                     _dup_heads(wb[:, qd + kd:], KV_HEADS)], axis=1).astype(BF16)
            gq = _permute_pairs_2d(jnp.tile(g_q_b[j], LANES // HEAD_DIM)).reshape(1, LANES)
            gk = _permute_pairs_2d(jnp.tile(g_k_b[j], LANES // HEAD_DIM)).reshape(1, LANES)
            proj = _inproj_b(x2, mod_mix, g_pre, w_cat, gq, gk, tab2, seq)
            o = _gqa_attention(proj, batch, seq)
            x2 = _outproj([o], None, x2, mod_mix, g_post, w_out_b[j].astype(BF16), seq)
        x2 = _moe(x2, mod_ffn, g_pre_ffn[i].reshape(1, d), g_post_ffn[i].reshape(1, d),
                  w_router[i], b_router[i], w_up, b_up[i], w_down, b_down[i], seq, i)
    return x2.reshape(batch, seq, d)
```
